```python
import functools
import jax, jax.numpy as jnp
from jax import lax
import numpy as np

D_MODEL = 1024
BATCH = 16
SEQ = 2048
DEPTH = 4
DEC_BATCH = 32
DEC_SEQ = 32
PAST_LEN = 2048

CHUNK = 64
LEFT_CHUNKS = 8
BAND_LEFT = LEFT_CHUNKS * CHUNK
BAND = BAND_LEFT + CHUNK
N_HEADS_A = 8
HEAD_DIM_A = 64
D_ATTN = N_HEADS_A * HEAD_DIM_A
REL_CLIP = 128
N_REL = REL_CLIP + CHUNK
N_HEADS_B = 8
HEAD_DIM_B = 64
D_RWKV = N_HEADS_B * HEAD_DIM_B
DECAY_RANK = 64
ICLR_RANK = 64
GATE_RANK = 128
SHIFT_COLS = 3 * D_RWKV + DECAY_RANK + ICLR_RANK + GATE_RANK
N_BRANCH = 2
PROJ_COLS = 3 * D_ATTN + SHIFT_COLS + N_BRANCH * D_MODEL
D_FF = -(-8 * D_MODEL // (3 * 256)) * 256
RMS_EPS = 1e-6
GN_EPS = 64e-5
NEG_INF = -1e30

kernel_name = 'chunk_hybrid_stream_step'


def rmsnorm(x, g):
    xf = x.astype(jnp.float32)
    y = xf * lax.rsqrt(jnp.mean(xf * xf, axis=-1, keepdims=True) + RMS_EPS)
    return (y * g.astype(jnp.float32)).astype(x.dtype)


def band_attend(q, k, v, q_pos, k_pos, valid, rel_bias):
    idx = jnp.clip(k_pos[None, :] - q_pos[:, None], -REL_CLIP, CHUNK - 1) + REL_CLIP
    bias = rel_bias[:, idx].astype(jnp.float32)
    s = jnp.einsum('bqhd,bkhd->bhqk', q, k).astype(jnp.float32) * (HEAD_DIM_A ** -0.5) + bias[None]
    s = jnp.where(valid[None, None, None, :], s, NEG_INF)
    p = jax.nn.softmax(s, axis=-1)
    return jnp.einsum('bhqk,bkhd->bqhd', p.astype(v.dtype), v)


def chunk_band_attention(q, k, v, rel_bias):
    B, T, H, Dh = q.shape
    nc = T // CHUNK
    pad = ((0, 0), (BAND_LEFT, 0), (0, 0), (0, 0))
    kp, vp = jnp.pad(k, pad), jnp.pad(v, pad)
    qc = jnp.moveaxis(q.reshape(B, nc, CHUNK, H, Dh), 1, 0)

    def one_chunk(args):
        q_c, ci = args
        start = ci * CHUNK
        kb = lax.dynamic_slice_in_dim(kp, start, BAND, axis=1)
        vb = lax.dynamic_slice_in_dim(vp, start, BAND, axis=1)
        q_pos = start + jnp.arange(CHUNK)
        k_pos = start - BAND_LEFT + jnp.arange(BAND)
        return band_attend(q_c, kb, vb, q_pos, k_pos, k_pos >= 0, rel_bias)

    o = lax.map(one_chunk, (qc, jnp.arange(nc)))
    return jnp.moveaxis(o, 0, 1).reshape(B, T, H, Dh)


def cached_band_attention(q, k, v, rel_bias, k_cache, v_cache):
    Ts = q.shape[1]
    Wc = k_cache.shape[1]
    kb = jnp.concatenate([k_cache.astype(k.dtype), k], axis=1)
    vb = jnp.concatenate([v_cache.astype(v.dtype), v], axis=1)
    q_pos = PAST_LEN + jnp.arange(Ts)
    k_pos = jnp.concatenate([PAST_LEN - Wc + jnp.arange(Wc), q_pos])
    valid = jnp.ones((Wc + Ts,), dtype=bool)
    return band_attend(q, kb, vb, q_pos, k_pos, valid, rel_bias)


def wkv7_scan(r, decay, k, v, kk, a, S0):
    def step(S, inp):
        r_t, w_t, k_t, v_t, kk_t, a_t = inp
        sa = -jnp.einsum('bhvk,bhk->bhv', S, kk_t)
        S = (S * w_t[:, :, None, :] + sa[..., None] * (kk_t * a_t)[:, :, None, :]
             + v_t[..., None] * k_t[:, :, None, :])
        return S, jnp.einsum('bhvk,bhk->bhv', S, r_t)

    xs = tuple(jnp.moveaxis(t, 1, 0) for t in (r, decay, k, v, kk, a))
    S, ys = lax.scan(step, S0, xs)
    return jnp.moveaxis(ys, 0, 1), S


def rwkv7_time_mix(zx, S0, w0, w_decay_up, a0, w_iclr_up, w_rg_up, k_k, k_a, r_k, gn_g, gn_b):
    B, T, _ = zx.shape
    cuts = [D_RWKV, 2 * D_RWKV, 3 * D_RWKV, 3 * D_RWKV + DECAY_RANK, 3 * D_RWKV + DECAY_RANK + ICLR_RANK]
    r, kx, v, wd, ad, gd = jnp.split(zx, cuts, axis=-1)
    f32 = lambda t: t.astype(jnp.float32)
    heads = lambda t: t.reshape(B, T, N_HEADS_B, HEAD_DIM_B)
    hshape = (N_HEADS_B, HEAD_DIM_B)
    w_log = -jax.nn.softplus(-f32(w0 + jnp.tanh(wd) @ w_decay_up)) - 0.5
    decay = jnp.exp(-jnp.exp(w_log))
    a = heads(jax.nn.sigmoid(f32(a0 + ad @ w_iclr_up)))
    g = jax.nn.sigmoid(gd) @ w_rg_up
    kk = heads(f32(kx * k_k))
    kk = kk * lax.rsqrt(jnp.maximum(jnp.sum(kk * kk, axis=-1, keepdims=True), 1e-24))
    k_h = heads(f32(kx)) * (1.0 + (a - 1.0) * f32(k_a).reshape(hshape))
    r_h, v_h = heads(f32(r)), heads(f32(v))
    y, S = wkv7_scan(r_h, heads(decay), k_h, v_h, kk, a, f32(S0))
    mean = jnp.mean(y, axis=-1, keepdims=True)
    var = jnp.mean(jnp.square(y - mean), axis=-1, keepdims=True)
    y = (y - mean) * lax.rsqrt(var + GN_EPS) * f32(gn_g).reshape(hshape) + f32(gn_b).reshape(hshape)
    y = y + jnp.sum(r_h * k_h * f32(r_k), axis=-1, keepdims=True) * v_h
    return y.reshape(B, T, D_RWKV).astype(zx.dtype) * g, S


def layer(x, c, attn_fn, wkv0, shift0, w_ada, b_ada, g_norm, w_in, rel_bias, mu, w0, w_decay_up,
          a0, w_iclr_up, w_rg_up, k_k, k_a, r_k, gn_g, gn_b, w_out_attn, w_out_rwkv, w_out,
          w_ffn_in, w_ffn_out):
    B, T, _ = x.shape
    mod = jax.nn.silu(c) @ w_ada + b_ada
    sh1, sc1, gt1, sh2, sc2, gt2 = jnp.split(mod[:, None, :], 6, axis=-1)
    h = rmsnorm(x, g_norm[0]) * (1 + sc1) + sh1
    z = h @ w_in
    qkv = z[..., :3 * D_ATTN].reshape(B, T, 3, N_HEADS_A, HEAD_DIM_A)
    q, k, v = qkv[:, :, 0], qkv[:, :, 1], qkv[:, :, 2]
    zs = z[..., 3 * D_ATTN:3 * D_ATTN + SHIFT_COLS]
    gates = jax.nn.sigmoid(z[..., 3 * D_ATTN + SHIFT_COLS:])
    o_a = attn_fn(q, k, v, rel_bias).reshape(B, T, D_ATTN)
    prev = jnp.concatenate([shift0[:, None, :].astype(zs.dtype), zs[:, :-1]], axis=1)
    zx = zs + (prev - zs) * mu
    o_b, wkv1 = rwkv7_time_mix(zx, wkv0, w0, w_decay_up, a0, w_iclr_up, w_rg_up, k_k, k_a, r_k, gn_g, gn_b)
    merged = gates[..., :D_MODEL] * (o_a @ w_out_attn) + gates[..., D_MODEL:] * (o_b @ w_out_rwkv)
    x = x + gt1 * (merged @ w_out)
    h2 = rmsnorm(x, g_norm[1]) * (1 + sc2) + sh2
    u, gg = jnp.split(h2 @ w_ffn_in, 2, axis=-1)
    x = x + gt2 * ((jax.nn.silu(gg) * u) @ w_ffn_out)
    return x, k, v, wkv1.astype(x.dtype), zs[:, -1]


def setup_inputs(seed: int = 0) -> dict:
    key = jax.random.key(seed)
    ks = iter(jax.random.split(key, 40))
    nrm = lambda shape, s: jax.random.normal(next(ks), shape, jnp.float32) * s
    W = min(BAND_LEFT, PAST_LEN)
    return {
        'x_prompt': nrm((BATCH, SEQ, D_MODEL), 1.0),
        'x_sample': nrm((DEC_BATCH, DEC_SEQ, D_MODEL), 1.0),
        'cache_attn_k': nrm((DEPTH, DEC_BATCH, W, N_HEADS_A, HEAD_DIM_A), 1.0),
        'cache_attn_v': nrm((DEPTH, DEC_BATCH, W, N_HEADS_A, HEAD_DIM_A), 1.0),
        'state_wkv': nrm((DEPTH, DEC_BATCH, N_HEADS_B, HEAD_DIM_B, HEAD_DIM_B), 0.5),
        'state_shift': nrm((DEPTH, DEC_BATCH, SHIFT_COLS), 1.0),
        'c_prompt': nrm((BATCH, D_MODEL), 1.0),
        'c_sample': nrm((DEC_BATCH, D_MODEL), 1.0),
        'w_ada': nrm((DEPTH, D_MODEL, 6 * D_MODEL), 0.5 * D_MODEL ** -0.5),
        'b_ada': nrm((DEPTH, 6 * D_MODEL), 0.02),
        'g_norm': 1.0 + nrm((DEPTH, 2, D_MODEL), 0.05),
        'w_in': nrm((DEPTH, D_MODEL, PROJ_COLS), D_MODEL ** -0.5),
        'rel_bias': nrm((DEPTH, N_HEADS_A, N_REL), 0.5),
        'mu': jax.random.uniform(next(ks), (DEPTH, SHIFT_COLS), jnp.float32),
        'w0': nrm((DEPTH, D_RWKV), 0.5),
        'w_decay_up': nrm((DEPTH, DECAY_RANK, D_RWKV), 0.5 * DECAY_RANK ** -0.5),
        'a0': nrm((DEPTH, D_RWKV), 0.5),
        'w_iclr_up': nrm((DEPTH, ICLR_RANK, D_RWKV), 0.5 * ICLR_RANK ** -0.5),
        'w_rg_up': nrm((DEPTH, GATE_RANK, D_RWKV), GATE_RANK ** -0.5),
        'k_k': 0.85 + nrm((DEPTH, D_RWKV), 0.1),
        'k_a': 1.0 + nrm((DEPTH, D_RWKV), 0.1),
        'r_k': nrm((DEPTH, N_HEADS_B, HEAD_DIM_B), 0.1),
        'gn_g': 1.0 + nrm((DEPTH, D_RWKV), 0.05),
        'gn_b': nrm((DEPTH, D_RWKV), 0.02),
        'w_out_attn': nrm((DEPTH, D_ATTN, D_MODEL), D_ATTN ** -0.5),
        'w_out_rwkv': nrm((DEPTH, D_RWKV, D_MODEL), D_RWKV ** -0.5),
        'w_out': nrm((DEPTH, D_MODEL, D_MODEL), D_MODEL ** -0.5),
        'w_ffn_in': nrm((DEPTH, D_MODEL, 2 * D_FF), D_MODEL ** -0.5),
        'w_ffn_out': nrm((DEPTH, D_FF, D_MODEL), D_FF ** -0.5),
        'g_final': 1.0 + nrm((D_MODEL,), 0.05),
    }


def reference(x_prompt, x_sample, cache_attn_k, cache_attn_v, state_wkv, state_shift, c_prompt, c_sample,
              w_ada, b_ada, g_norm, w_in, rel_bias, mu, w0, w_decay_up, a0, w_iclr_up, w_rg_up, k_k, k_a,
              r_k, gn_g, gn_b, w_out_attn, w_out_rwkv, w_out, w_ffn_in, w_ffn_out, g_final):
    Bp, T, _ = x_prompt.shape
    n_keep = min(BAND_LEFT, T)
    zero_S = jnp.zeros((Bp, N_HEADS_B, HEAD_DIM_B, HEAD_DIM_B), jnp.float32)
    zero_shift = jnp.zeros((Bp, SHIFT_COLS), x_prompt.dtype)
    xp, xs = x_prompt, x_sample
    kp_l, vp_l, Sp_l, shp_l, ks_l, vs_l, Ss_l, shs_l = [], [], [], [], [], [], [], []
    for l in range(DEPTH):
        lw = (w_ada[l], b_ada[l], g_norm[l], w_in[l], rel_bias[l], mu[l], w0[l], w_decay_up[l], a0[l],
              w_iclr_up[l], w_rg_up[l], k_k[l], k_a[l], r_k[l], gn_g[l], gn_b[l], w_out_attn[l],
              w_out_rwkv[l], w_out[l], w_ffn_in[l], w_ffn_out[l])
        xp, k, v, S, sh = layer(xp, c_prompt, chunk_band_attention, zero_S, zero_shift, *lw)
        kp_l.append(k[:, T - n_keep:])
        vp_l.append(v[:, T - n_keep:])
        Sp_l.append(S)
        shp_l.append(sh)
        sample_attn = functools.partial(cached_band_attention, k_cache=cache_attn_k[l], v_cache=cache_attn_v[l])
        xs, k, v, S, sh = layer(xs, c_sample, sample_attn, state_wkv[l], state_shift[l], *lw)
        ks_l.append(k)
        vs_l.append(v)
        Ss_l.append(S)
        shs_l.append(sh)
    y_prompt = rmsnorm(xp, g_final)
    y_sample = rmsnorm(xs, g_final)
    new_k_prompt = jnp.stack(kp_l)
    new_v_prompt = jnp.stack(vp_l)
    new_wkv_prompt = jnp.stack(Sp_l)
    new_shift_prompt = jnp.stack(shp_l)
    new_k_sample = jnp.stack(ks_l)
    new_v_sample = jnp.stack(vs_l)
    new_wkv_sample = jnp.stack(Ss_l)
    new_shift_sample = jnp.stack(shs_l)
    return (y_prompt, y_sample, new_k_prompt, new_v_prompt, new_wkv_prompt, new_shift_prompt,
            new_k_sample, new_v_sample, new_wkv_sample, new_shift_sample)
```

```python
import functools

import jax
import jax.numpy as jnp
from jax import lax
from jax.experimental import pallas as pl
from jax.experimental.pallas import tpu as pltpu

D_MODEL = 1024
DEPTH = 4
CHUNK = 64
BAND_LEFT = 512
BAND = BAND_LEFT + CHUNK
N_HEADS = 8
HEAD_DIM = 64
D_ATTN = 512
D_RWKV = 512
REL_CLIP = 128
DECAY_RANK = 64
ICLR_RANK = 64
GATE_RANK = 128
SHIFT_COLS = 3 * D_RWKV + DECAY_RANK + ICLR_RANK + GATE_RANK
PROJ_COLS = 3 * D_ATTN + SHIFT_COLS + 2 * D_MODEL
D_FF = 2816
RMS_EPS = 1e-6
GN_EPS = 64e-5
NEG_INF = -1e30

F32 = jnp.float32
BF16 = jnp.bfloat16

VMEM_LIMIT_BYTES = 56 * 1024 * 1024


def _cparams(*sem):
    return pltpu.CompilerParams(dimension_semantics=sem, vmem_limit_bytes=VMEM_LIMIT_BYTES)


def _const_spec(shape):
    nd = len(shape)
    return pl.BlockSpec(shape, lambda *_: (0,) * nd, pipeline_mode=pl.Buffered(1))


def _dot(a, b):
    return jnp.dot(a, b, preferred_element_type=F32)


def _dot_nt(a, b):
    return lax.dot_general(a, b, (((1,), (1,)), ((), ())), preferred_element_type=F32)


def _dot_tn(a, b):
    return lax.dot_general(a, b, (((0,), (0,)), ((), ())), preferred_element_type=F32)


def _sigmoid(x):
    return 1.0 / (1.0 + jnp.exp(-x))


def _expand_mod(ref, nb, tt):
    m = ref[...]
    if nb == 1:
        return m[0]
    return jnp.broadcast_to(m, (nb, tt, m.shape[-1])).reshape(nb * tt, m.shape[-1])


def _rms(x, g):
    ms = jnp.mean(x * x, axis=-1, keepdims=True)
    return x * lax.rsqrt(ms + RMS_EPS) * g


def _mod_kernel(c_ref, w_ref, b_ref, o_ref):
    c = c_ref[...]
    s = (c * _sigmoid(c)).astype(BF16)
    o_ref[0] = _dot(s, w_ref[0].astype(BF16)) + b_ref[0]


def _modulation(c_all, w_ada, b_ada):
    nb = c_all.shape[0]
    tn = 1536
    return pl.pallas_call(
        _mod_kernel,
        grid=(DEPTH, 6 * D_MODEL // tn),
        in_specs=[
            pl.BlockSpec((nb, D_MODEL), lambda l, j: (0, 0)),
            pl.BlockSpec((1, D_MODEL, tn), lambda l, j: (l, 0, j)),
            pl.BlockSpec((1, 1, tn), lambda l, j: (l, 0, j)),
        ],
        out_specs=pl.BlockSpec((1, nb, tn), lambda l, j: (l, 0, j)),
        out_shape=jax.ShapeDtypeStruct((DEPTH, nb, 6 * D_MODEL), F32),
        compiler_params=_cparams("parallel", "parallel"),
        name="adaln_mod",
    )(c_all, w_ada, b_ada.reshape(DEPTH, 1, 6 * D_MODEL))


_Q0, _K0, _V0, _ZS0, _GA0, _GB0 = 0, 512, 1024, 1536, 3328, 4352
_COL_STEP = 512


def _proj_in_kernel(x_ref, sh_ref, sc_ref, g_ref, w_ref, kin_ref, vin_ref,
                    q_ref, k_ref, v_ref, kb_ref, vb_ref, zs_ref, ga_ref, gb_ref, *, nb, tt):
    del kin_ref, vin_ref
    x = x_ref[...]
    h = _rms(x, g_ref[...]) * (1.0 + _expand_mod(sc_ref, nb, tt)) + _expand_mod(sh_ref, nb, tt)
    hb = h.astype(BF16)

    def cols(a, b):
        return _dot(hb, w_ref[:, a:b])

    q_ref[...] = (cols(_Q0, _K0) * (HEAD_DIM ** -0.5)).astype(BF16)
    k = cols(_K0, _V0)
    k_ref[...] = k.reshape(k_ref.shape)
    kb_ref[...] = k.astype(BF16).reshape(kb_ref.shape)
    v = cols(_V0, _ZS0)
    v_ref[...] = v.reshape(v_ref.shape)
    vb_ref[...] = v.astype(BF16).reshape(vb_ref.shape)
    for a in range(_ZS0, _GA0, _COL_STEP):
        b = min(a + _COL_STEP, _GA0)
        zs_ref[:, a - _ZS0:b - _ZS0] = cols(a, b)
    for a in range(_GA0, _GB0, _COL_STEP):
        ga_ref[:, a - _GA0:a - _GA0 + _COL_STEP] = _sigmoid(cols(a, a + _COL_STEP)).astype(BF16)
    for a in range(_GB0, PROJ_COLS, _COL_STEP):
        gb_ref[:, a - _GB0:a - _GB0 + _COL_STEP] = _sigmoid(cols(a, a + _COL_STEP)).astype(BF16)


def _proj_in(x2, sh, sc, g, w_in_b, kbuf, vbuf, *, B, T, nb, tt):
    M = B * T
    tm = nb * tt
    n_keep = min(BAND_LEFT, T)
    assert tt == n_keep and T % tt == 0 and B % nb == 0 and BAND_LEFT % tt == 0
    assert nb == 1 or tt == T
    nt = T // tt
    pad_blocks = BAND_LEFT // tt
    row = lambda b, t: (b * nt + t, 0)
    modspec = pl.BlockSpec((nb, 1, D_MODEL), lambda b, t: (b, 0, 0))
    out_shapes = (
        jax.ShapeDtypeStruct((M, D_ATTN), BF16),
        jax.ShapeDtypeStruct((B, n_keep, D_ATTN), F32),
        jax.ShapeDtypeStruct((B, n_keep, D_ATTN), F32),
        jax.ShapeDtypeStruct(kbuf.shape, BF16),
        jax.ShapeDtypeStruct(vbuf.shape, BF16),
        jax.ShapeDtypeStruct((M, SHIFT_COLS), F32),
        jax.ShapeDtypeStruct((M, D_MODEL), BF16),
        jax.ShapeDtypeStruct((M, D_MODEL), BF16),
    )
    tail = pl.BlockSpec((nb, tt, D_ATTN), lambda b, t: (b, 0, 0))
    behind = pl.BlockSpec((nb, tt, D_ATTN), lambda b, t: (b, t + pad_blocks, 0))
    return pl.pallas_call(
        functools.partial(_proj_in_kernel, nb=nb, tt=tt),
        grid=(B // nb, nt),
        in_specs=[
            pl.BlockSpec((tm, D_MODEL), row), modspec, modspec, _const_spec((1, D_MODEL)),
            _const_spec((D_MODEL, PROJ_COLS)),
            pl.BlockSpec(memory_space=pl.ANY), pl.BlockSpec(memory_space=pl.ANY),
        ],
        out_specs=(
            pl.BlockSpec((tm, D_ATTN), row), tail, tail, behind, behind,
            pl.BlockSpec((tm, SHIFT_COLS), row), pl.BlockSpec((tm, D_MODEL), row), pl.BlockSpec((tm, D_MODEL), row),
        ),
        out_shape=out_shapes,
        input_output_aliases={5: 3, 6: 4},
        compiler_params=_cparams("parallel", "arbitrary"),
        name="proj_in",
    )(x2, sh, sc, g, w_in_b, kbuf, vbuf)


def _attn_kernel(q_ref, k_ref, v_ref, bias_ref, o_ref, *, tq, n_sub, band, masked):
    j = pl.program_id(1)
    for i in range(n_sub):
        c = j * n_sub + i
        start = pl.multiple_of(c * tq, tq)
        q = q_ref[i * tq:(i + 1) * tq, :]
        kb = k_ref[0, pl.ds(start, band), :]
        vb = v_ref[0, pl.ds(start, band), :]
        if masked:
            col = lax.broadcasted_iota(jnp.int32, (1, band), 1)
            valid = col >= BAND_LEFT - start
        outs = []
        for h in range(N_HEADS):
            sl = slice(h * HEAD_DIM, (h + 1) * HEAD_DIM)
            s = _dot_nt(q[:, sl], kb[:, sl]) + bias_ref[h]
            if masked:
                s = jnp.where(valid, s, NEG_INF)
            m = jnp.max(s, axis=-1, keepdims=True)
            p = jnp.exp(s - m)
            l = jnp.sum(p, axis=-1, keepdims=True)
            o = _dot(p.astype(BF16), vb[:, sl])
            outs.append(o / l)
        o_ref[i * tq:(i + 1) * tq, :] = jnp.concatenate(outs, axis=-1).astype(BF16)


def _attention(q2, kbuf, vbuf, bias, *, B, T, tq, n_sub, masked):
    band = bias.shape[-1]
    Tp = kbuf.shape[1]
    nj = T // (tq * n_sub)
    row = lambda b, j: (b * nj + j, 0)
    kv = pl.BlockSpec((1, Tp, D_ATTN), lambda b, j: (b, 0, 0))
    return pl.pallas_call(
        functools.partial(_attn_kernel, tq=tq, n_sub=n_sub, band=band, masked=masked),
        grid=(B, nj),
        in_specs=[pl.BlockSpec((tq * n_sub, D_ATTN), row), kv, kv, _const_spec(bias.shape)],
        out_specs=pl.BlockSpec((tq * n_sub, D_ATTN), row),
        out_shape=jax.ShapeDtypeStruct((B * T, D_ATTN), BF16),
        compiler_params=_cparams("parallel", "arbitrary"),
        name="band_attn",
    )(q2, kbuf, vbuf, bias)


def _split_bf16(x, n):
    parts = []
    for _ in range(n):
        p = x.astype(BF16)
        parts.append(p)
        x = x - p.astype(F32)
    return parts


def _rwkv_kernel(zs_ref, sh0_ref, s0_ref, mu_ref, w0_ref, wdu_ref, a0_ref, wiu_ref, wrg_ref, kk_ref, ka_ref,
                 rk_ref, gng_ref, gnb_ref, ones_ref, tri_ref,
                 ob_ref, sout_ref, carry_ref, st_ref, *, TT, C):
    t = pl.program_id(1)

    @pl.when(t == 0)
    def _():
        carry_ref[...] = sh0_ref[0]
        st_ref[...] = s0_ref[0]

    zs = zs_ref[...]
    rows = lax.broadcasted_iota(jnp.int32, (TT, 1), 0)
    prev = jnp.where(rows == 0, carry_ref[...], pltpu.roll(zs, 1, 0))
    carry_ref[...] = zs[TT - 1:TT, :]
    zx = zs + (prev - zs) * mu_ref[...]

    r = zx[:, 0:D_RWKV]
    kx = zx[:, D_RWKV:2 * D_RWKV]
    v = zx[:, 2 * D_RWKV:3 * D_RWKV]
    o = 3 * D_RWKV
    wd = zx[:, o:o + DECAY_RANK]
    ad = zx[:, o + DECAY_RANK:o + DECAY_RANK + ICLR_RANK]
    gd = zx[:, o + DECAY_RANK + ICLR_RANK:]

    ones_bd = ones_ref[...]

    def segsum(x):
        hi, lo = _split_bf16(x, 2)
        return _dot(hi, ones_bd) + _dot(lo, ones_bd)

    xw = w0_ref[...] + _dot(jnp.tanh(wd).astype(BF16), wdu_ref[...])
    w_log = -(jnp.maximum(-xw, 0.0) + jnp.log(1.0 + jnp.exp(-jnp.abs(xw)))) - 0.5
    lw = -jnp.exp(w_log)
    a = _sigmoid(a0_ref[...] + _dot(ad.astype(BF16), wiu_ref[...]))
    g = _dot(_sigmoid(gd).astype(BF16), wrg_ref[...])
    kkv = kx * kk_ref[...]
    kap = kkv * lax.rsqrt(jnp.maximum(segsum(kkv * kkv), 1e-24))
    kh = kx * (1.0 + (a - 1.0) * ka_ref[...])
    bb = kap * a

    tri = tri_ref[...]
    cum = None
    for part in _split_bf16(lw, 3):
        d = _dot(tri, part)
        cum = d if cum is None else cum + d

    ecp = jnp.exp(cum - lw)
    kt_all = kap * ecp
    rt_all = r * (ecp * jnp.exp(lw))
    en = jnp.exp(-cum)
    kh_all = kh * en
    bh_all = bb * en

    ii = lax.broadcasted_iota(jnp.int32, (C, C), 0)
    jj = lax.broadcasted_iota(jnp.int32, (C, C), 1)
    strict = ii > jj
    incl = ii >= jj
    eye = (ii == jj).astype(F32)
    eye_k = (lax.broadcasted_iota(jnp.int32, (HEAD_DIM, HEAD_DIM), 0)
             == lax.broadcasted_iota(jnp.int32, (HEAD_DIM, HEAD_DIM), 1)).astype(F32)
    n_sq = C.bit_length() - 2

    y_chunks = []
    for c in range(TT // C):
        rs = slice(c * C, (c + 1) * C)
        cum_end = cum[(c + 1) * C - 1:(c + 1) * C, :]
        ed = jnp.exp(cum_end - cum[rs])
        pc = jnp.exp(cum_end)
        kd_c = (kh[rs] * ed).astype(BF16)
        bd_c = (bb[rs] * ed).astype(BF16)
        v_c = v[rs].astype(BF16)
        kt_c = kt_all[rs]
        rt_c = rt_all[rs]
        kt_cb = kt_c.astype(BF16)
        x_c = jnp.concatenate([kt_cb, rt_c.astype(BF16)], axis=0)
        khb = kh_all[rs].astype(BF16)
        bhb = bh_all[rs].astype(BF16)
        y_heads = []
        for h in range(N_HEADS):
            sl = slice(h * HEAD_DIM, (h + 1) * HEAD_DIM)
            xk = _dot_nt(x_c[:, sl], khb[:, sl])
            xb = _dot_nt(x_c[:, sl], bhb[:, sl])
            mvk = jnp.where(strict, xk[:C], 0.0).astype(BF16)
            ark = jnp.where(incl, xk[C:], 0.0).astype(BF16)
            lm = jnp.where(strict, xb[:C], 0.0)
            arb = jnp.where(incl, xb[C:], 0.0).astype(BF16)
            tm = eye - lm
            p = lm
            for _ in range(n_sq):
                pb = p.astype(BF16)
                p = _dot(pb, pb)
                tm = tm + _dot(tm.astype(BF16), p.astype(BF16))
            tmb = tm.astype(BF16)
            vh = v_c[:, sl]
            mv = _dot(mvk, vh)
            tk = _dot(tmb, kt_cb[:, sl]).astype(BF16)
            tmv = _dot(tmb, mv.astype(BF16)).astype(BF16)
            gm = eye_k * pc[:, sl] - _dot_tn(bd_c[:, sl], tk)
            hm = _dot_tn(kd_c[:, sl], vh) - _dot_tn(bd_c[:, sl], tmv)
            qp = rt_c[:, sl] - _dot(arb, tk)
            y0 = _dot(ark, vh) - _dot(arb, tmv)
            st = st_ref[h]
            stb = st.astype(BF16)
            y_heads.append(_dot(qp.astype(BF16), stb) + y0)
            st_ref[h] = _dot(gm.astype(BF16), stb) + hm
        y_chunks.append(jnp.concatenate(y_heads, axis=-1))
    y = y_chunks[0] if len(y_chunks) == 1 else jnp.concatenate(y_chunks, axis=0)

    inv_n = 1.0 / HEAD_DIM
    mean = segsum(y) * inv_n
    dlt = y - mean
    var = segsum(dlt * dlt) * inv_n
    yn = dlt * lax.rsqrt(var + GN_EPS) * gng_ref[...] + gnb_ref[...]
    yn = yn + segsum(r * kh * rk_ref[...]) * v
    ob_ref[...] = (yn * g).astype(BF16)
    sout_ref[0] = st_ref[...]


def _rwkv(zs2, shift0, st0, mu, w0, wdu_b, a0, wiu_b, wrg_b, k_k, k_a, r_k, gn_g, gn_b, ones_bd, tri, *, B, T, TT, C):
    nt = T // TT
    vec = lambda n: _const_spec((1, n))
    row = lambda b, t: (b * nt + t, 0)
    return pl.pallas_call(
        functools.partial(_rwkv_kernel, TT=TT, C=C),
        grid=(B, nt),
        in_specs=[
            pl.BlockSpec((TT, SHIFT_COLS), row),
            pl.BlockSpec((1, 1, SHIFT_COLS), lambda b, t: (b, 0, 0)),
            pl.BlockSpec((1, N_HEADS, HEAD_DIM, HEAD_DIM), lambda b, t: (b, 0, 0, 0)),
            vec(SHIFT_COLS), vec(D_RWKV), _const_spec((DECAY_RANK, D_RWKV)), vec(D_RWKV),
            _const_spec((ICLR_RANK, D_RWKV)), _const_spec((GATE_RANK, D_RWKV)),
            vec(D_RWKV), vec(D_RWKV), vec(D_RWKV), vec(D_RWKV), vec(D_RWKV),
            _const_spec((D_RWKV, D_RWKV)), _const_spec((TT, TT)),
        ],
        out_specs=(
            pl.BlockSpec((TT, D_RWKV), row),
            pl.BlockSpec((1, N_HEADS, HEAD_DIM, HEAD_DIM), lambda b, t: (b, 0, 0, 0)),
        ),
        out_shape=(
            jax.ShapeDtypeStruct((B * T, D_RWKV), BF16),
            jax.ShapeDtypeStruct((B, N_HEADS, HEAD_DIM, HEAD_DIM), F32),
        ),
        scratch_shapes=[pltpu.VMEM((1, SHIFT_COLS), F32), pltpu.VMEM((N_HEADS, HEAD_DIM, HEAD_DIM), F32)],
        compiler_params=_cparams("parallel", "arbitrary"),
        name="rwkv7",
    )(zs2, shift0, st0, mu, w0, wdu_b, a0, wiu_b, wrg_b, k_k, k_a, r_k, gn_g, gn_b, ones_bd, tri)


def _merge_kernel(x_ref, oa_ref, ob_ref, ga_ref, gb_ref, gt_ref, woa_ref, wob_ref, wo_ref, o_ref, *, nb, tt):
    ma = ga_ref[...].astype(F32) * _dot(oa_ref[...], woa_ref[...])
    mb = gb_ref[...].astype(F32) * _dot(ob_ref[...], wob_ref[...])
    merged = (ma + mb).astype(BF16)
    o_ref[...] = x_ref[...] + _expand_mod(gt_ref, nb, tt) * _dot(merged, wo_ref[...])


def _merge(x2, oa, ob, ga, gb, gt, woa_b, wob_b, wo_b, *, B, T, nb, tt):
    M = B * T
    tm = nb * tt
    nt = T // tt
    row = lambda b, t: (b * nt + t, 0)
    act = lambda n: pl.BlockSpec((tm, n), row)
    return pl.pallas_call(
        functools.partial(_merge_kernel, nb=nb, tt=tt),
        grid=(B // nb, nt),
        in_specs=[
            act(D_MODEL), act(D_ATTN), act(D_RWKV), act(D_MODEL), act(D_MODEL),
            pl.BlockSpec((nb, 1, D_MODEL), lambda b, t: (b, 0, 0)),
            _const_spec((D_ATTN, D_MODEL)), _const_spec((D_RWKV, D_MODEL)), _const_spec((D_MODEL, D_MODEL)),
        ],
        out_specs=act(D_MODEL),
        out_shape=jax.ShapeDtypeStruct((M, D_MODEL), F32),
        compiler_params=_cparams("parallel", "parallel"),
        name="merge_out",
    )(x2, oa, ob, ga, gb, gt, woa_b, wob_b, wo_b)


_FF_STEP = D_FF // 2


def _ffn_kernel(x_ref, sh_ref, sc_ref, gt_ref, g_ref, gf_ref, wi_ref, wo_ref, o_ref, *, nb, tt, final):
    x = x_ref[...]
    h = _rms(x, g_ref[...]) * (1.0 + _expand_mod(sc_ref, nb, tt)) + _expand_mod(sh_ref, nb, tt)
    hb = h.astype(BF16)
    acc = None
    for a in range(0, D_FF, _FF_STEP):
        u = _dot(hb, wi_ref[:, a:a + _FF_STEP])
        gg = _dot(hb, wi_ref[:, D_FF + a:D_FF + a + _FF_STEP])
        act = (gg * _sigmoid(gg) * u).astype(BF16)
        d = _dot(act, wo_ref[a:a + _FF_STEP, :])
        acc = d if acc is None else acc + d
    y = x + _expand_mod(gt_ref, nb, tt) * acc
    if final:
        y = _rms(y, gf_ref[...])
    o_ref[...] = y


def _ffn(x2, sh, sc, gt, g, g_final, wi_b, wo_b, *, B, T, nb, tt, final):
    M = B * T
    tm = nb * tt
    nt = T // tt
    row = lambda b, t: (b * nt + t, 0)
    modspec = pl.BlockSpec((nb, 1, D_MODEL), lambda b, t: (b, 0, 0))
    return pl.pallas_call(
        functools.partial(_ffn_kernel, nb=nb, tt=tt, final=final),
        grid=(B // nb, nt),
        in_specs=[
            pl.BlockSpec((tm, D_MODEL), row), modspec, modspec, modspec,
            _const_spec((1, D_MODEL)), _const_spec((1, D_MODEL)),
            _const_spec((D_MODEL, 2 * D_FF)), _const_spec((D_FF, D_MODEL)),
        ],
        out_specs=pl.BlockSpec((tm, D_MODEL), row),
        out_shape=jax.ShapeDtypeStruct((M, D_MODEL), F32),
        compiler_params=_cparams("parallel", "parallel"),
        name="ffn",
    )(x2, sh, sc, gt, g, g_final, wi_b, wo_b)


def _bias_table(rel_bias, tq, band):
    rel = (jnp.arange(band)[None, :] - BAND_LEFT) - jnp.arange(tq)[:, None]
    idx = jnp.clip(rel, -REL_CLIP, CHUNK - 1) + REL_CLIP
    return rel_bias[:, idx].astype(F32)


def _tile_consts(TT, C):
    i = jnp.arange(TT)
    tri = ((i[:, None] >= i[None, :]) & (i[:, None] // C == i[None, :] // C)).astype(BF16)
    j = jnp.arange(D_RWKV)
    ones_bd = (j[:, None] // HEAD_DIM == j[None, :] // HEAD_DIM).astype(BF16)
    return ones_bd, tri


def _layer(x2, mods, kbuf, vbuf, shift0, st0, lw, g_final, *, B, T, nb, tt, tq, n_sub, masked, TT, C, final):
    sh1, sc1, gt1, sh2, sc2, gt2 = mods
    q, k_tail, v_tail, kbuf, vbuf, zs, ga, gb = _proj_in(
        x2, sh1, sc1, lw['g1'], lw['w_in'], kbuf, vbuf, B=B, T=T, nb=nb, tt=tt)
    oa = _attention(q, kbuf, vbuf, lw['bias'], B=B, T=T, tq=tq, n_sub=n_sub, masked=masked)
    ob, st1 = _rwkv(zs, shift0, st0, lw['mu'], lw['w0'], lw['wdu'], lw['a0'], lw['wiu'], lw['wrg'], lw['k_k'],
                    lw['k_a'], lw['r_k'], lw['gn_g'], lw['gn_b'], lw['ones_bd'], lw['tri'], B=B, T=T, TT=TT, C=C)
    x2 = _merge(x2, oa, ob, ga, gb, gt1, lw['woa'], lw['wob'], lw['wo'], B=B, T=T, nb=nb, tt=tt)
    x2 = _ffn(x2, sh2, sc2, gt2, lw['g2'], g_final, lw['wfi'], lw['wfo'], B=B, T=T, nb=nb, tt=tt, final=final)
    return x2, k_tail, v_tail, kbuf, vbuf, st1, zs


def _path_config(B, T):
    n_keep = min(BAND_LEFT, T)
    tt = n_keep
    nb = 1 if T > tt else min(B, max(1, 256 // tt))
    if T >= CHUNK:
        tq, n_sub, masked = CHUNK, min(4, T // CHUNK), True
    else:
        tq, n_sub, masked = T, 1, False
    TT = min(256, T)
    C = min(CHUNK, TT)
    return dict(nb=nb, tt=tt, tq=tq, n_sub=n_sub, masked=masked, TT=TT, C=C)


def _layer_weights(l, p, cfgs):
    row = lambda a: a.reshape(1, -1)
    lw = dict(
        g1=row(p['g_norm'][l, 0]), g2=row(p['g_norm'][l, 1]),
        w_in=p['w_in'][l].astype(BF16),
        mu=row(p['mu'][l]), w0=row(p['w0'][l]), wdu=p['w_decay_up'][l].astype(BF16), a0=row(p['a0'][l]),
        wiu=p['w_iclr_up'][l].astype(BF16), wrg=p['w_rg_up'][l].astype(BF16),
        k_k=row(p['k_k'][l]), k_a=row(p['k_a'][l]), r_k=row(p['r_k'][l]),
        gn_g=row(p['gn_g'][l]), gn_b=row(p['gn_b'][l]),
        woa=p['w_out_attn'][l].astype(BF16), wob=p['w_out_rwkv'][l].astype(BF16), wo=p['w_out'][l].astype(BF16),
        wfi=p['w_ffn_in'][l].astype(BF16), wfo=p['w_ffn_out'][l].astype(BF16),
    )
    out = []
    for cfg in cfgs:
        d = dict(lw)
        band = BAND_LEFT + cfg['tq']
        d['bias'] = _bias_table(p['rel_bias'][l], cfg['tq'], band)
        d['ones_bd'], d['tri'] = _tile_consts(cfg['TT'], cfg['C'])
        out.append(d)
    return out


def _split_mods(mod, lo, hi):
    m = mod[lo:hi].reshape(hi - lo, 1, 6, D_MODEL)
    return tuple(m[:, :, i, :] for i in range(6))


def kernel(x_prompt, x_sample, cache_attn_k, cache_attn_v, state_wkv, state_shift, c_prompt, c_sample, w_ada, b_ada, g_norm, w_in, rel_bias, mu, w0, w_decay_up, a0, w_iclr_up, w_rg_up, k_k, k_a, r_k, gn_g, gn_b, w_out_attn, w_out_rwkv, w_out, w_ffn_in, w_ffn_out, g_final):
    p = dict(g_norm=g_norm, w_in=w_in, rel_bias=rel_bias, mu=mu, w0=w0, w_decay_up=w_decay_up, a0=a0,
             w_iclr_up=w_iclr_up, w_rg_up=w_rg_up, k_k=k_k, k_a=k_a, r_k=r_k, gn_g=gn_g, gn_b=gn_b,
             w_out_attn=w_out_attn, w_out_rwkv=w_out_rwkv, w_out=w_out, w_ffn_in=w_ffn_in, w_ffn_out=w_ffn_out)
    Bp, T, _ = x_prompt.shape
    Bs, Ts, _ = x_sample.shape
    W = cache_attn_k.shape[2]
    assert W == BAND_LEFT
    cfg_p, cfg_s = _path_config(Bp, T), _path_config(Bs, Ts)
    gf = g_final.reshape(1, D_MODEL)

    mod = _modulation(jnp.concatenate([c_prompt, c_sample], axis=0), w_ada, b_ada)

    xp = x_prompt.reshape(Bp * T, D_MODEL)
    xs = x_sample.reshape(Bs * Ts, D_MODEL)
    kbuf_p = jnp.zeros((Bp, BAND_LEFT + T, D_ATTN), BF16)
    vbuf_p = jnp.zeros((Bp, BAND_LEFT + T, D_ATTN), BF16)
    zero_shift = jnp.zeros((Bp, 1, SHIFT_COLS), F32)
    zero_state = jnp.zeros((Bp, N_HEADS, HEAD_DIM, HEAD_DIM), F32)
    n_keep = min(BAND_LEFT, T)

    outs = [[] for _ in range(8)]
    for l in range(DEPTH):
        lw_p, lw_s = _layer_weights(l, p, (cfg_p, cfg_s))
        final = l == DEPTH - 1
        xp, k_t, v_t, kbuf_p, vbuf_p, st, zs = _layer(
            xp, _split_mods(mod[l], 0, Bp), kbuf_p, vbuf_p, zero_shift, zero_state, lw_p, gf,
            B=Bp, T=T, final=final, **cfg_p)
        outs[0].append(k_t.reshape(Bp, n_keep, N_HEADS, HEAD_DIM))
        outs[1].append(v_t.reshape(Bp, n_keep, N_HEADS, HEAD_DIM))
        outs[2].append(jnp.swapaxes(st, -1, -2))
        outs[3].append(zs.reshape(Bp, T, SHIFT_COLS)[:, -1])

        pad = ((0, 0), (0, Ts), (0, 0))
        kbuf_s = jnp.pad(cache_attn_k[l].reshape(Bs, W, D_ATTN).astype(BF16), pad)
        vbuf_s = jnp.pad(cache_attn_v[l].reshape(Bs, W, D_ATTN).astype(BF16), pad)
        xs, k_t, v_t, _, _, st, zs = _layer(
            xs, _split_mods(mod[l], Bp, Bp + Bs), kbuf_s, vbuf_s, state_shift[l].reshape(Bs, 1, SHIFT_COLS),
            jnp.swapaxes(state_wkv[l], -1, -2), lw_s, gf, B=Bs, T=Ts, final=final, **cfg_s)
        outs[4].append(k_t.reshape(Bs, Ts, N_HEADS, HEAD_DIM))
        outs[5].append(v_t.reshape(Bs, Ts, N_HEADS, HEAD_DIM))
        outs[6].append(jnp.swapaxes(st, -1, -2))
        outs[7].append(zs.reshape(Bs, Ts, SHIFT_COLS)[:, -1])

    y_prompt = xp.reshape(Bp, T, D_MODEL)
    y_sample = xs.reshape(Bs, Ts, D_MODEL)
    return (y_prompt, y_sample) + tuple(jnp.stack(o) for o in outs)
```

```python
import functools

import jax
import jax.numpy as jnp
from jax import lax
from jax.experimental import pallas as pl
from jax.experimental.pallas import tpu as pltpu

D_MODEL = 1024
DEPTH = 4
CHUNK = 64
BAND_LEFT = 512
BAND = BAND_LEFT + CHUNK
N_HEADS = 8
HEAD_DIM = 64
D_ATTN = 512
D_RWKV = 512
REL_CLIP = 128
DECAY_RANK = 64
ICLR_RANK = 64
GATE_RANK = 128
SHIFT_COLS = 3 * D_RWKV + DECAY_RANK + ICLR_RANK + GATE_RANK
PROJ_COLS = 3 * D_ATTN + SHIFT_COLS + 2 * D_MODEL
D_FF = 2816
RMS_EPS = 1e-6
GN_EPS = 64e-5
NEG_INF = -1e30

F32 = jnp.float32
BF16 = jnp.bfloat16

VMEM_LIMIT_BYTES = 56 * 1024 * 1024


def _cparams(*sem):
    return pltpu.CompilerParams(dimension_semantics=sem, vmem_limit_bytes=VMEM_LIMIT_BYTES)


def _const_spec(shape):
    nd = len(shape)
    return pl.BlockSpec(shape, lambda *_: (0,) * nd, pipeline_mode=pl.Buffered(1))


def _dot(a, b):
    return jnp.dot(a, b, preferred_element_type=F32)


def _dot_nt(a, b):
    return lax.dot_general(a, b, (((1,), (1,)), ((), ())), preferred_element_type=F32)


def _dot_tn(a, b):
    return lax.dot_general(a, b, (((0,), (0,)), ((), ())), preferred_element_type=F32)


def _sigmoid(x):
    return 1.0 / (1.0 + jnp.exp(-x))


def _expand_mod(ref, nb, tt):
    m = ref[...]
    if nb == 1:
        return m[0]
    return jnp.broadcast_to(m, (nb, tt, m.shape[-1])).reshape(nb * tt, m.shape[-1])


def _rms(x, g):
    ms = jnp.mean(x * x, axis=-1, keepdims=True)
    return x * lax.rsqrt(ms + RMS_EPS) * g


def _mod_kernel(c_ref, w_ref, b_ref, o_ref):
    c = c_ref[...]
    s = (c * _sigmoid(c)).astype(BF16)
    o_ref[0] = _dot(s, w_ref[0].astype(BF16)) + b_ref[0]


def _modulation(c_all, w_ada, b_ada):
    nb = c_all.shape[0]
    tn = 1536
    return pl.pallas_call(
        _mod_kernel,
        grid=(DEPTH, 6 * D_MODEL // tn),
        in_specs=[
            pl.BlockSpec((nb, D_MODEL), lambda l, j: (0, 0)),
            pl.BlockSpec((1, D_MODEL, tn), lambda l, j: (l, 0, j)),
            pl.BlockSpec((1, 1, tn), lambda l, j: (l, 0, j)),
        ],
        out_specs=pl.BlockSpec((1, nb, tn), lambda l, j: (l, 0, j)),
        out_shape=jax.ShapeDtypeStruct((DEPTH, nb, 6 * D_MODEL), F32),
        compiler_params=_cparams("parallel", "parallel"),
        name="adaln_mod",
    )(c_all, w_ada, b_ada.reshape(DEPTH, 1, 6 * D_MODEL))


_Q0, _K0, _V0, _ZS0, _GA0, _GB0 = 0, 512, 1024, 1536, 3328, 4352
_COL_STEP = 512


def _proj_in_kernel(x_ref, sh_ref, sc_ref, g_ref, w_ref, kin_ref, vin_ref,
                    q_ref, k_ref, v_ref, kb_ref, vb_ref, zs_ref, ga_ref, gb_ref, *, nb, tt):
    del kin_ref, vin_ref
    x = x_ref[...]
    h = _rms(x, g_ref[...]) * (1.0 + _expand_mod(sc_ref, nb, tt)) + _expand_mod(sh_ref, nb, tt)
    hb = h.astype(BF16)

    def cols(a, b):
        return _dot(hb, w_ref[:, a:b])

    q_ref[...] = (cols(_Q0, _K0) * (HEAD_DIM ** -0.5)).astype(BF16)
    k = cols(_K0, _V0)
    k_ref[...] = k.reshape(k_ref.shape)
    kb_ref[...] = k.astype(BF16).reshape(kb_ref.shape)
    v = cols(_V0, _ZS0)
    v_ref[...] = v.reshape(v_ref.shape)
    vb_ref[...] = v.astype(BF16).reshape(vb_ref.shape)
    for a in range(_ZS0, _GA0, _COL_STEP):
        b = min(a + _COL_STEP, _GA0)
        zs_ref[:, a - _ZS0:b - _ZS0] = cols(a, b)
    for a in range(_GA0, _GB0, _COL_STEP):
        ga_ref[:, a - _GA0:a - _GA0 + _COL_STEP] = _sigmoid(cols(a, a + _COL_STEP)).astype(BF16)
    for a in range(_GB0, PROJ_COLS, _COL_STEP):
        gb_ref[:, a - _GB0:a - _GB0 + _COL_STEP] = _sigmoid(cols(a, a + _COL_STEP)).astype(BF16)


def _proj_in(x2, sh, sc, g, w_in_b, kbuf, vbuf, *, B, T, nb, tt):
    M = B * T
    tm = nb * tt
    n_keep = min(BAND_LEFT, T)
    assert tt == n_keep and T % tt == 0 and B % nb == 0 and BAND_LEFT % tt == 0
    assert nb == 1 or tt == T
    nt = T // tt
    pad_blocks = BAND_LEFT // tt
    row = lambda b, t: (b * nt + t, 0)
    modspec = pl.BlockSpec((nb, 1, D_MODEL), lambda b, t: (b, 0, 0))
    out_shapes = (
        jax.ShapeDtypeStruct((M, D_ATTN), BF16),
        jax.ShapeDtypeStruct((B, n_keep, D_ATTN), F32),
        jax.ShapeDtypeStruct((B, n_keep, D_ATTN), F32),
        jax.ShapeDtypeStruct(kbuf.shape, BF16),
        jax.ShapeDtypeStruct(vbuf.shape, BF16),
        jax.ShapeDtypeStruct((M, SHIFT_COLS), F32),
        jax.ShapeDtypeStruct((M, D_MODEL), BF16),
        jax.ShapeDtypeStruct((M, D_MODEL), BF16),
    )
    tail = pl.BlockSpec((nb, tt, D_ATTN), lambda b, t: (b, 0, 0))
    behind = pl.BlockSpec((nb, tt, D_ATTN), lambda b, t: (b, t + pad_blocks, 0))
    return pl.pallas_call(
        functools.partial(_proj_in_kernel, nb=nb, tt=tt),
        grid=(B // nb, nt),
        in_specs=[
            pl.BlockSpec((tm, D_MODEL), row), modspec, modspec, _const_spec((1, D_MODEL)),
            _const_spec((D_MODEL, PROJ_COLS)),
            pl.BlockSpec(memory_space=pl.ANY), pl.BlockSpec(memory_space=pl.ANY),
        ],
        out_specs=(
            pl.BlockSpec((tm, D_ATTN), row), tail, tail, behind, behind,
            pl.BlockSpec((tm, SHIFT_COLS), row), pl.BlockSpec((tm, D_MODEL), row), pl.BlockSpec((tm, D_MODEL), row),
        ),
        out_shape=out_shapes,
        input_output_aliases={5: 3, 6: 4},
        compiler_params=_cparams("parallel", "arbitrary"),
        name="proj_in",
    )(x2, sh, sc, g, w_in_b, kbuf, vbuf)


def _attn_kernel(q_ref, k_ref, v_ref, bias_ref, o_ref, *, nbb, tq, n_sub, band, masked):
    j = pl.program_id(1)
    sls = [slice(h * HEAD_DIM, (h + 1) * HEAD_DIM) for h in range(N_HEADS)]
    qs, kbs, vbs, valids = [], [], [], []
    for bb in range(nbb):
        for i in range(n_sub):
            start = pl.multiple_of((j * n_sub + i) * tq, tq)
            r0 = (bb * n_sub + i) * tq
            qs.append(q_ref[r0:r0 + tq, :])
            kbs.append(k_ref[bb, pl.ds(start, band), :])
            vbs.append(v_ref[bb, pl.ds(start, band), :])
            valids.append(lax.broadcasted_iota(jnp.int32, (1, band), 1) >= BAND_LEFT - start)
    n = nbb * n_sub
    ss = [[_dot_nt(qs[i][:, sl], kbs[i][:, sl]) + bias_ref[h] for h, sl in enumerate(sls)] for i in range(n)]
    if masked:
        ss = [[jnp.where(valids[i], s, NEG_INF) for s in ss[i]] for i in range(n)]
    ps = [[jnp.exp(s - jnp.max(s, axis=-1, keepdims=True)) for s in ss[i]] for i in range(n)]
    ls = [[jnp.sum(p, axis=-1, keepdims=True) for p in ps[i]] for i in range(n)]
    os_ = [[_dot(ps[i][h].astype(BF16), vbs[i][:, sl]) for h, sl in enumerate(sls)] for i in range(n)]
    for i in range(n):
        o = jnp.concatenate([os_[i][h] / ls[i][h] for h in range(N_HEADS)], axis=-1)
        o_ref[i * tq:(i + 1) * tq, :] = o.astype(BF16)


def _attention(q2, kbuf, vbuf, bias, *, B, T, nbb, tq, n_sub, masked):
    band = bias.shape[-1]
    Tp = kbuf.shape[1]
    nj = T // (tq * n_sub)
    assert nbb == 1 or nj == 1
    rows = nbb * tq * n_sub
    row = lambda b, j: (b * nj + j, 0)
    kv = pl.BlockSpec((nbb, Tp, D_ATTN), lambda b, j: (b, 0, 0))
    return pl.pallas_call(
        functools.partial(_attn_kernel, nbb=nbb, tq=tq, n_sub=n_sub, band=band, masked=masked),
        grid=(B // nbb, nj),
        in_specs=[pl.BlockSpec((rows, D_ATTN), row), kv, kv, _const_spec(bias.shape)],
        out_specs=pl.BlockSpec((rows, D_ATTN), row),
        out_shape=jax.ShapeDtypeStruct((B * T, D_ATTN), BF16),
        compiler_params=_cparams("parallel", "arbitrary"),
        name="band_attn",
    )(q2, kbuf, vbuf, bias)


def _split_bf16(x, n):
    parts = []
    for _ in range(n):
        p = x.astype(BF16)
        parts.append(p)
        x = x - p.astype(F32)
    return parts


def _rwkv_kernel(zs_ref, sh0_ref, s0_ref, mu_ref, w0_ref, wdu_ref, a0_ref, wiu_ref, wrg_ref, kk_ref, ka_ref,
                 rk_ref, gng_ref, gnb_ref, ones_ref, tri_ref,
                 ob_ref, sout_ref, carry_ref, st_ref, *, TT, C, per_row):
    zs = zs_ref[...]
    rows = lax.broadcasted_iota(jnp.int32, (TT, 1), 0)
    if per_row:
        first = _expand_mod(sh0_ref, TT // C, C)
        prev = jnp.where(rows % C == 0, first, pltpu.roll(zs, 1, 0))
    else:
        @pl.when(pl.program_id(1) == 0)
        def _():
            carry_ref[...] = sh0_ref[0]
            st_ref[...] = s0_ref[0]

        prev = jnp.where(rows == 0, carry_ref[...], pltpu.roll(zs, 1, 0))
        carry_ref[...] = zs[TT - 1:TT, :]
    zx = zs + (prev - zs) * mu_ref[...]

    r = zx[:, 0:D_RWKV]
    kx = zx[:, D_RWKV:2 * D_RWKV]
    v = zx[:, 2 * D_RWKV:3 * D_RWKV]
    o = 3 * D_RWKV
    wd = zx[:, o:o + DECAY_RANK]
    ad = zx[:, o + DECAY_RANK:o + DECAY_RANK + ICLR_RANK]
    gd = zx[:, o + DECAY_RANK + ICLR_RANK:]

    ones_bd = ones_ref[...]

    def segsum(x):
        hi, lo = _split_bf16(x, 2)
        return _dot(hi, ones_bd) + _dot(lo, ones_bd)

    xw = w0_ref[...] + _dot(jnp.tanh(wd).astype(BF16), wdu_ref[...])
    w_log = -(jnp.maximum(-xw, 0.0) + jnp.log(1.0 + jnp.exp(-jnp.abs(xw)))) - 0.5
    lw = -jnp.exp(w_log)
    a = _sigmoid(a0_ref[...] + _dot(ad.astype(BF16), wiu_ref[...]))
    g = _dot(_sigmoid(gd).astype(BF16), wrg_ref[...])
    kkv = kx * kk_ref[...]
    kap = kkv * lax.rsqrt(jnp.maximum(segsum(kkv * kkv), 1e-24))
    kh = kx * (1.0 + (a - 1.0) * ka_ref[...])
    bb = kap * a

    tri = tri_ref[...]
    cum = None
    for part in _split_bf16(lw, 3):
        d = _dot(tri, part)
        cum = d if cum is None else cum + d

    ecp = jnp.exp(cum - lw)
    kt_all = kap * ecp
    rt_all = r * (ecp * jnp.exp(lw))
    en = jnp.exp(-cum)
    kh_all = kh * en
    bh_all = bb * en

    n_ch = TT // C
    ii = lax.broadcasted_iota(jnp.int32, (TT, TT), 0)
    jj = lax.broadcasted_iota(jnp.int32, (TT, TT), 1)
    incl = tri > 0
    strict = incl & (ii != jj)
    eye = (ii == jj).astype(F32)
    eye_k = (lax.broadcasted_iota(jnp.int32, (HEAD_DIM, HEAD_DIM), 0)
             == lax.broadcasted_iota(jnp.int32, (HEAD_DIM, HEAD_DIM), 1)).astype(F32)
    n_sq = C.bit_length() - 2

    ed_rows, pcs = [], []
    for c in range(n_ch):
        cum_end = cum[(c + 1) * C - 1:(c + 1) * C, :]
        ed_rows.append(jnp.exp(cum_end - cum[c * C:(c + 1) * C]))
        pcs.append(jnp.exp(cum_end))
    ed = ed_rows[0] if n_ch == 1 else jnp.concatenate(ed_rows, axis=0)
    kd_b = (kh * ed).astype(BF16)
    bd_b = (bb * ed).astype(BF16)
    v_b = v.astype(BF16)
    kt_b = kt_all.astype(BF16)
    x_b = jnp.concatenate([kt_b, rt_all.astype(BF16)], axis=0)
    khb = kh_all.astype(BF16)
    bhb = bh_all.astype(BF16)

    heads = range(N_HEADS)
    sls = [slice(h * HEAD_DIM, (h + 1) * HEAD_DIM) for h in heads]
    vhs = [v_b[:, sl] for sl in sls]
    xks = [_dot_nt(x_b[:, sl], khb[:, sl]) for sl in sls]
    xbs = [_dot_nt(x_b[:, sl], bhb[:, sl]) for sl in sls]
    kvs = [[_dot_tn(kd_b[c * C:(c + 1) * C, sl], vhs[h][c * C:(c + 1) * C]) for c in range(n_ch)]
           for h, sl in zip(heads, sls)]
    mvks = [jnp.where(strict, xk[:TT], 0.0).astype(BF16) for xk in xks]
    arks = [jnp.where(incl, xk[TT:], 0.0).astype(BF16) for xk in xks]
    ps = [jnp.where(strict, xb[:TT], 0.0) for xb in xbs]
    arbs = [jnp.where(incl, xb[TT:], 0.0).astype(BF16) for xb in xbs]
    mvs = [_dot(mvks[h], vhs[h]).astype(BF16) for h in heads]
    avs = [_dot(arks[h], vhs[h]) for h in heads]
    tms = [eye - p for p in ps]
    for _ in range(n_sq):
        pbs = [p.astype(BF16) for p in ps]
        ps = [_dot(pb, pb) for pb in pbs]
        tms = [tms[h] + _dot(tms[h].astype(BF16), ps[h].astype(BF16)) for h in heads]
    tws = [_dot(tms[h].astype(BF16), jnp.concatenate([kt_b[:, sls[h]], mvs[h]], axis=-1)).astype(BF16)
           for h in heads]
    abs_ = [_dot(arbs[h], tws[h]) for h in heads]
    qps = [(rt_all[:, sls[h]] - abs_[h][:, :HEAD_DIM]).astype(BF16) for h in heads]
    y0s = [avs[h] - abs_[h][:, HEAD_DIM:] for h in heads]
    ghs = [[_dot_tn(bd_b[c * C:(c + 1) * C, sls[h]], tws[h][c * C:(c + 1) * C]) for c in range(n_ch)]
           for h in heads]
    sts = None if per_row else [st_ref[h] for h in heads]
    ys = [[] for _ in heads]
    for c in range(n_ch):
        rs = slice(c * C, (c + 1) * C)
        for h in heads:
            gm = eye_k * pcs[c][:, sls[h]] - ghs[h][c][:, :HEAD_DIM]
            hm = kvs[h][c] - ghs[h][c][:, HEAD_DIM:]
            qg = jnp.concatenate([qps[h][rs], gm.astype(BF16)], axis=0)
            st = s0_ref[c, h] if per_row else sts[h]
            r2 = _dot(qg, st.astype(BF16))
            ys[h].append(r2[:C] + y0s[h][rs])
            if per_row:
                sout_ref[c, h] = r2[C:] + hm
            else:
                sts[h] = r2[C:] + hm
    if not per_row:
        for h in heads:
            st_ref[h] = sts[h]
            sout_ref[0, h] = sts[h]
    y = jnp.concatenate([yh[0] if n_ch == 1 else jnp.concatenate(yh, axis=0) for yh in ys], axis=-1)

    inv_n = 1.0 / HEAD_DIM
    mean = segsum(y) * inv_n
    dlt = y - mean
    var = segsum(dlt * dlt) * inv_n
    yn = dlt * lax.rsqrt(var + GN_EPS) * gng_ref[...] + gnb_ref[...]
    yn = yn + segsum(r * kh * rk_ref[...]) * v
    ob_ref[...] = (yn * g).astype(BF16)


def _rwkv(zs2, shift0, st0, mu, w0, wdu_b, a0, wiu_b, wrg_b, k_k, k_a, r_k, gn_g, gn_b, ones_bd, tri, *, B, T, TT, C):
    per_row = C == T
    assert (TT % T == 0 and B % (TT // T) == 0) if per_row else T % TT == 0
    nbr = TT // C if per_row else 1
    nt = 1 if per_row else T // TT
    vec = lambda n: _const_spec((1, n))
    row = lambda b, t: (b * nt + t, 0)
    return pl.pallas_call(
        functools.partial(_rwkv_kernel, TT=TT, C=C, per_row=per_row),
        grid=(B // nbr, nt),
        in_specs=[
            pl.BlockSpec((TT, SHIFT_COLS), row),
            pl.BlockSpec((nbr, 1, SHIFT_COLS), lambda b, t: (b, 0, 0)),
            pl.BlockSpec((nbr, N_HEADS, HEAD_DIM, HEAD_DIM), lambda b, t: (b, 0, 0, 0)),
            vec(SHIFT_COLS), vec(D_RWKV), _const_spec((DECAY_RANK, D_RWKV)), vec(D_RWKV),
            _const_spec((ICLR_RANK, D_RWKV)), _const_spec((GATE_RANK, D_RWKV)),
            vec(D_RWKV), vec(D_RWKV), vec(D_RWKV), vec(D_RWKV), vec(D_RWKV),
            _const_spec((D_RWKV, D_RWKV)), _const_spec((TT, TT)),
        ],
        out_specs=(
            pl.BlockSpec((TT, D_RWKV), row),
            pl.BlockSpec((nbr, N_HEADS, HEAD_DIM, HEAD_DIM), lambda b, t: (b, 0, 0, 0)),
        ),
        out_shape=(
            jax.ShapeDtypeStruct((B * T, D_RWKV), BF16),
            jax.ShapeDtypeStruct((B, N_HEADS, HEAD_DIM, HEAD_DIM), F32),
        ),
        scratch_shapes=[pltpu.VMEM((1, SHIFT_COLS), F32), pltpu.VMEM((N_HEADS, HEAD_DIM, HEAD_DIM), F32)],
        compiler_params=_cparams("parallel", "arbitrary"),
        name="rwkv7",
    )(zs2, shift0, st0, mu, w0, wdu_b, a0, wiu_b, wrg_b, k_k, k_a, r_k, gn_g, gn_b, ones_bd, tri)


def _merge_kernel(x_ref, oa_ref, ob_ref, ga_ref, gb_ref, gt_ref, woa_ref, wob_ref, wo_ref, o_ref, *, nb, tt):
    ma = ga_ref[...].astype(F32) * _dot(oa_ref[...], woa_ref[...])
    mb = gb_ref[...].astype(F32) * _dot(ob_ref[...], wob_ref[...])
    merged = (ma + mb).astype(BF16)
    o_ref[...] = x_ref[...] + _expand_mod(gt_ref, nb, tt) * _dot(merged, wo_ref[...])


def _merge(x2, oa, ob, ga, gb, gt, woa_b, wob_b, wo_b, *, B, T, nb, tt):
    M = B * T
    tm = nb * tt
    nt = T // tt
    row = lambda b, t: (b * nt + t, 0)
    act = lambda n: pl.BlockSpec((tm, n), row)
    return pl.pallas_call(
        functools.partial(_merge_kernel, nb=nb, tt=tt),
        grid=(B // nb, nt),
        in_specs=[
            act(D_MODEL), act(D_ATTN), act(D_RWKV), act(D_MODEL), act(D_MODEL),
            pl.BlockSpec((nb, 1, D_MODEL), lambda b, t: (b, 0, 0)),
            _const_spec((D_ATTN, D_MODEL)), _const_spec((D_RWKV, D_MODEL)), _const_spec((D_MODEL, D_MODEL)),
        ],
        out_specs=act(D_MODEL),
        out_shape=jax.ShapeDtypeStruct((M, D_MODEL), F32),
        compiler_params=_cparams("parallel", "parallel"),
        name="merge_out",
    )(x2, oa, ob, ga, gb, gt, woa_b, wob_b, wo_b)


_FF_STEP = D_FF // 2


def _ffn_kernel(x_ref, sh_ref, sc_ref, gt_ref, g_ref, gf_ref, wi_ref, wo_ref, o_ref, *, nb, tt, final):
    x = x_ref[...]
    h = _rms(x, g_ref[...]) * (1.0 + _expand_mod(sc_ref, nb, tt)) + _expand_mod(sh_ref, nb, tt)
    hb = h.astype(BF16)
    acc = None
    for a in range(0, D_FF, _FF_STEP):
        u = _dot(hb, wi_ref[:, a:a + _FF_STEP])
        gg = _dot(hb, wi_ref[:, D_FF + a:D_FF + a + _FF_STEP])
        act = (gg * _sigmoid(gg) * u).astype(BF16)
        d = _dot(act, wo_ref[a:a + _FF_STEP, :])
        acc = d if acc is None else acc + d
    y = x + _expand_mod(gt_ref, nb, tt) * acc
    if final:
        y = _rms(y, gf_ref[...])
    o_ref[...] = y


def _ffn(x2, sh, sc, gt, g, g_final, wi_b, wo_b, *, B, T, nb, tt, final):
    M = B * T
    tm = nb * tt
    nt = T // tt
    row = lambda b, t: (b * nt + t, 0)
    modspec = pl.BlockSpec((nb, 1, D_MODEL), lambda b, t: (b, 0, 0))
    return pl.pallas_call(
        functools.partial(_ffn_kernel, nb=nb, tt=tt, final=final),
        grid=(B // nb, nt),
        in_specs=[
            pl.BlockSpec((tm, D_MODEL), row), modspec, modspec, modspec,
            _const_spec((1, D_MODEL)), _const_spec((1, D_MODEL)),
            _const_spec((D_MODEL, 2 * D_FF)), _const_spec((D_FF, D_MODEL)),
        ],
        out_specs=pl.BlockSpec((tm, D_MODEL), row),
        out_shape=jax.ShapeDtypeStruct((M, D_MODEL), F32),
        compiler_params=_cparams("parallel", "parallel"),
        name="ffn",
    )(x2, sh, sc, gt, g, g_final, wi_b, wo_b)


def _bias_kernel(rb_ref, o_ref, *, tq, band):
    n_rel = rb_ref.shape[1]
    parts = _split_bf16(rb_ref[...], 3)
    col = lax.broadcasted_iota(jnp.int32, (n_rel, band), 1)
    bucket = lax.broadcasted_iota(jnp.int32, (n_rel, band), 0)
    for q in range(tq):
        idx = jnp.clip(col - (BAND_LEFT + q), -REL_CLIP, CHUNK - 1) + REL_CLIP
        onehot = jnp.where(idx == bucket, 1.0, 0.0).astype(BF16)
        o_ref[q] = (_dot(parts[0], onehot) + _dot(parts[1], onehot)) + _dot(parts[2], onehot)


def _bias_tables(rel_bias, tq, band):
    depth, n_heads, n_rel = rel_bias.shape
    r = depth * n_heads
    t = pl.pallas_call(
        functools.partial(_bias_kernel, tq=tq, band=band),
        out_shape=jax.ShapeDtypeStruct((tq, r, band), F32),
        compiler_params=pltpu.CompilerParams(vmem_limit_bytes=VMEM_LIMIT_BYTES),
        name="bias_table",
    )(rel_bias.reshape(r, n_rel))
    return jnp.transpose(t, (1, 0, 2)).reshape(depth, n_heads, tq, band)


def _tile_consts(TT, C):
    i = jnp.arange(TT)
    tri = ((i[:, None] >= i[None, :]) & (i[:, None] // C == i[None, :] // C)).astype(BF16)
    j = jnp.arange(D_RWKV)
    ones_bd = (j[:, None] // HEAD_DIM == j[None, :] // HEAD_DIM).astype(BF16)
    return ones_bd, tri


def _layer(x2, mods, kbuf, vbuf, shift0, st0, lw, g_final, *, B, T, nb, tt, nbb, tq, n_sub, masked, TT, C, final):
    sh1, sc1, gt1, sh2, sc2, gt2 = mods
    q, k_tail, v_tail, kbuf, vbuf, zs, ga, gb = _proj_in(
        x2, sh1, sc1, lw['g1'], lw['w_in'], kbuf, vbuf, B=B, T=T, nb=nb, tt=tt)
    oa = _attention(q, kbuf, vbuf, lw['bias'], B=B, T=T, nbb=nbb, tq=tq, n_sub=n_sub, masked=masked)
    ob, st1 = _rwkv(zs, shift0, st0, lw['mu'], lw['w0'], lw['wdu'], lw['a0'], lw['wiu'], lw['wrg'], lw['k_k'],
                    lw['k_a'], lw['r_k'], lw['gn_g'], lw['gn_b'], lw['ones_bd'], lw['tri'], B=B, T=T, TT=TT, C=C)
    x2 = _merge(x2, oa, ob, ga, gb, gt1, lw['woa'], lw['wob'], lw['wo'], B=B, T=T, nb=nb, tt=tt)
    x2 = _ffn(x2, sh2, sc2, gt2, lw['g2'], g_final, lw['wfi'], lw['wfo'], B=B, T=T, nb=nb, tt=tt, final=final)
    return x2, k_tail, v_tail, kbuf, vbuf, st1, zs


def _path_config(B, T):
    n_keep = min(BAND_LEFT, T)
    tt = n_keep
    nb = 1 if T > tt else min(B, max(1, 256 // tt))
    if T >= CHUNK:
        nbb, tq, n_sub, masked = 1, CHUNK, min(4, T // CHUNK), True
    else:
        nbb, tq, n_sub, masked = min(B, 8), T, 1, False
    if T >= 256:
        TT, C = 256, CHUNK
    else:
        C = T
        TT = C * min(B, max(1, 256 // C))
    return dict(nb=nb, tt=tt, nbb=nbb, tq=tq, n_sub=n_sub, masked=masked, TT=TT, C=C)


def _layer_weights(l, p, cfgs, bias_all):
    row = lambda a: a.reshape(1, -1)
    lw = dict(
        g1=row(p['g_norm'][l, 0]), g2=row(p['g_norm'][l, 1]),
        w_in=p['w_in'][l].astype(BF16),
        mu=row(p['mu'][l]), w0=row(p['w0'][l]), wdu=p['w_decay_up'][l].astype(BF16), a0=row(p['a0'][l]),
        wiu=p['w_iclr_up'][l].astype(BF16), wrg=p['w_rg_up'][l].astype(BF16),
        k_k=row(p['k_k'][l]), k_a=row(p['k_a'][l]), r_k=row(p['r_k'][l]),
        gn_g=row(p['gn_g'][l]), gn_b=row(p['gn_b'][l]),
        woa=p['w_out_attn'][l].astype(BF16), wob=p['w_out_rwkv'][l].astype(BF16), wo=p['w_out'][l].astype(BF16),
        wfi=p['w_ffn_in'][l].astype(BF16), wfo=p['w_ffn_out'][l].astype(BF16),
    )
    out = []
    for cfg in cfgs:
        d = dict(lw)
        d['bias'] = bias_all[l, :, :cfg['tq'], :BAND_LEFT + cfg['tq']]
        d['ones_bd'], d['tri'] = _tile_consts(cfg['TT'], cfg['C'])
        out.append(d)
    return out


def _split_mods(mod, lo, hi):
    m = mod[lo:hi].reshape(hi - lo, 1, 6, D_MODEL)
    return tuple(m[:, :, i, :] for i in range(6))


def kernel(x_prompt, x_sample, cache_attn_k, cache_attn_v, state_wkv, state_shift, c_prompt, c_sample, w_ada, b_ada, g_norm, w_in, rel_bias, mu, w0, w_decay_up, a0, w_iclr_up, w_rg_up, k_k, k_a, r_k, gn_g, gn_b, w_out_attn, w_out_rwkv, w_out, w_ffn_in, w_ffn_out, g_final):
    p = dict(g_norm=g_norm, w_in=w_in, rel_bias=rel_bias, mu=mu, w0=w0, w_decay_up=w_decay_up, a0=a0,
             w_iclr_up=w_iclr_up, w_rg_up=w_rg_up, k_k=k_k, k_a=k_a, r_k=r_k, gn_g=gn_g, gn_b=gn_b,
             w_out_attn=w_out_attn, w_out_rwkv=w_out_rwkv, w_out=w_out, w_ffn_in=w_ffn_in, w_ffn_out=w_ffn_out)
    Bp, T, _ = x_prompt.shape
    Bs, Ts, _ = x_sample.shape
    W = cache_attn_k.shape[2]
    assert W == BAND_LEFT
    cfg_p, cfg_s = _path_config(Bp, T), _path_config(Bs, Ts)
    gf = g_final.reshape(1, D_MODEL)

    mod = _modulation(jnp.concatenate([c_prompt, c_sample], axis=0), w_ada, b_ada)
    bias_all = _bias_tables(rel_bias, CHUNK, BAND)

    xp = x_prompt.reshape(Bp * T, D_MODEL)
    xs = x_sample.reshape(Bs * Ts, D_MODEL)
    kbuf_p = jnp.zeros((Bp, BAND_LEFT + T, D_ATTN), BF16)
    vbuf_p = jnp.zeros((Bp, BAND_LEFT + T, D_ATTN), BF16)
    zero_shift = jnp.zeros((Bp, 1, SHIFT_COLS), F32)
    zero_state = jnp.zeros((Bp, N_HEADS, HEAD_DIM, HEAD_DIM), F32)
    n_keep = min(BAND_LEFT, T)

    outs = [[] for _ in range(8)]
    for l in range(DEPTH):
        lw_p, lw_s = _layer_weights(l, p, (cfg_p, cfg_s), bias_all)
        final = l == DEPTH - 1
        xp, k_t, v_t, kbuf_p, vbuf_p, st, zs = _layer(
            xp, _split_mods(mod[l], 0, Bp), kbuf_p, vbuf_p, zero_shift, zero_state, lw_p, gf,
            B=Bp, T=T, final=final, **cfg_p)
        outs[0].append(k_t.reshape(Bp, n_keep, N_HEADS, HEAD_DIM))
        outs[1].append(v_t.reshape(Bp, n_keep, N_HEADS, HEAD_DIM))
        outs[2].append(jnp.swapaxes(st, -1, -2))
        outs[3].append(zs.reshape(Bp, T, SHIFT_COLS)[:, -1])

        pad = ((0, 0), (0, Ts), (0, 0))
        kbuf_s = jnp.pad(cache_attn_k[l].reshape(Bs, W, D_ATTN).astype(BF16), pad)
        vbuf_s = jnp.pad(cache_attn_v[l].reshape(Bs, W, D_ATTN).astype(BF16), pad)
        xs, k_t, v_t, _, _, st, zs = _layer(
            xs, _split_mods(mod[l], Bp, Bp + Bs), kbuf_s, vbuf_s, state_shift[l].reshape(Bs, 1, SHIFT_COLS),
            jnp.swapaxes(state_wkv[l], -1, -2), lw_s, gf, B=Bs, T=Ts, final=final, **cfg_s)
        outs[4].append(k_t.reshape(Bs, Ts, N_HEADS, HEAD_DIM))
        outs[5].append(v_t.reshape(Bs, Ts, N_HEADS, HEAD_DIM))
        outs[6].append(jnp.swapaxes(st, -1, -2))
        outs[7].append(zs.reshape(Bs, Ts, SHIFT_COLS)[:, -1])

    y_prompt = xp.reshape(Bp, T, D_MODEL)
    y_sample = xs.reshape(Bs, Ts, D_MODEL)
    return (y_prompt, y_sample) + tuple(jnp.stack(o) for o in outs)
```

```python
import functools

import jax
import jax.numpy as jnp
from jax import lax
from jax.experimental import pallas as pl
from jax.experimental.pallas import tpu as pltpu

D_MODEL = 1024
DEPTH = 4
CHUNK = 64
BAND_LEFT = 512
BAND = BAND_LEFT + CHUNK
N_HEADS = 8
HEAD_DIM = 64
D_ATTN = 512
D_RWKV = 512
REL_CLIP = 128
DECAY_RANK = 64
ICLR_RANK = 64
GATE_RANK = 128
SHIFT_COLS = 3 * D_RWKV + DECAY_RANK + ICLR_RANK + GATE_RANK
PROJ_COLS = 3 * D_ATTN + SHIFT_COLS + 2 * D_MODEL
D_FF = 2816
RMS_EPS = 1e-6
GN_EPS = 64e-5
NEG_INF = -1e30
LOG2E = 1.4426950408889634

F32 = jnp.float32
BF16 = jnp.bfloat16

RWKV_TILE_ROWS = 512
RWKV_ROW_CHUNKS = 4
RWKV_MAX_BATCH_ROWS = 8
ATTN_GROUP_CHUNKS = 2
ATTN_STEP_ROWS = 256
SUM_TERMS = 1
CUM_TERMS = 2
CUM_BLOCK = 256

GH = 4
GW = GH * HEAD_DIM
NG = N_HEADS // GH

VMEM_LIMIT_BYTES = 56 * 1024 * 1024


def _cparams(*sem):
    return pltpu.CompilerParams(dimension_semantics=sem, vmem_limit_bytes=VMEM_LIMIT_BYTES)


def _const_spec(shape):
    nd = len(shape)
    return pl.BlockSpec(shape, lambda *_: (0,) * nd, pipeline_mode=pl.Buffered(1))


def _dot(a, b):
    return jnp.dot(a, b, preferred_element_type=F32)


def _dot_nt(a, b):
    return lax.dot_general(a, b, (((1,), (1,)), ((), ())), preferred_element_type=F32)


def _dot_tn(a, b):
    return lax.dot_general(a, b, (((0,), (0,)), ((), ())), preferred_element_type=F32)


def _sigmoid(x):
    return 1.0 / (1.0 + jnp.exp(-x))


def _expand_mod(ref, nb, tt):
    m = ref[...]
    if nb == 1:
        return m[0]
    return jnp.broadcast_to(m, (nb, tt, m.shape[-1])).reshape(nb * tt, m.shape[-1])


def _rms(x, g):
    ms = jnp.mean(x * x, axis=-1, keepdims=True)
    return x * lax.rsqrt(ms + RMS_EPS) * g


def _mod_kernel(c_ref, w_ref, b_ref, o_ref):
    c = c_ref[...]
    s = (c * _sigmoid(c)).astype(BF16)
    o_ref[0] = _dot(s, w_ref[0].astype(BF16)) + b_ref[0]


def _modulation(c_all, w_ada, b_ada):
    nb = c_all.shape[0]
    tn = 1536
    return pl.pallas_call(
        _mod_kernel,
        grid=(DEPTH, 6 * D_MODEL // tn),
        in_specs=[
            pl.BlockSpec((nb, D_MODEL), lambda l, j: (0, 0)),
            pl.BlockSpec((1, D_MODEL, tn), lambda l, j: (l, 0, j)),
            pl.BlockSpec((1, 1, tn), lambda l, j: (l, 0, j)),
        ],
        out_specs=pl.BlockSpec((1, nb, tn), lambda l, j: (l, 0, j)),
        out_shape=jax.ShapeDtypeStruct((DEPTH, nb, 6 * D_MODEL), F32),
        compiler_params=_cparams("parallel", "parallel"),
        name="adaln_mod",
    )(c_all, w_ada, b_ada.reshape(DEPTH, 1, 6 * D_MODEL))


_Q0, _K0, _V0, _ZS0, _GA0, _GB0 = 0, 512, 1024, 1536, 3328, 4352
_COL_STEP = 512


def _proj_in_kernel(x_ref, sh_ref, sc_ref, g_ref, w_ref, kin_ref, vin_ref,
                    q_ref, k_ref, v_ref, kb_ref, vb_ref, zs_ref, ga_ref, gb_ref, *, nb, tt):
    del kin_ref, vin_ref
    x = x_ref[...]
    h = _rms(x, g_ref[...]) * (1.0 + _expand_mod(sc_ref, nb, tt)) + _expand_mod(sh_ref, nb, tt)
    hb = h.astype(BF16)

    def cols(a, b):
        return _dot(hb, w_ref[:, a:b])

    q_ref[...] = (cols(_Q0, _K0) * (HEAD_DIM ** -0.5 * LOG2E)).astype(BF16)
    k = cols(_K0, _V0)
    k_ref[...] = k.reshape(k_ref.shape)
    kb_ref[...] = k.astype(BF16).reshape(kb_ref.shape)
    v = cols(_V0, _ZS0)
    v_ref[...] = v.reshape(v_ref.shape)
    vb_ref[...] = v.astype(BF16).reshape(vb_ref.shape)
    for a in range(_ZS0, _GA0, _COL_STEP):
        b = min(a + _COL_STEP, _GA0)
        zs_ref[:, a - _ZS0:b - _ZS0] = cols(a, b)
    for a in range(_GA0, _GB0, _COL_STEP):
        ga_ref[:, a - _GA0:a - _GA0 + _COL_STEP] = _sigmoid(cols(a, a + _COL_STEP)).astype(BF16)
    for a in range(_GB0, PROJ_COLS, _COL_STEP):
        gb_ref[:, a - _GB0:a - _GB0 + _COL_STEP] = _sigmoid(cols(a, a + _COL_STEP)).astype(BF16)


def _proj_in(x2, sh, sc, g, w_in_b, kbuf, vbuf, *, B, T, nb, tt):
    M = B * T
    tm = nb * tt
    n_keep = min(BAND_LEFT, T)
    assert tt == n_keep and T % tt == 0 and B % nb == 0 and BAND_LEFT % tt == 0
    assert nb == 1 or tt == T
    nt = T // tt
    pad_blocks = BAND_LEFT // tt
    row = lambda b, t: (b * nt + t, 0)
    modspec = pl.BlockSpec((nb, 1, D_MODEL), lambda b, t: (b, 0, 0))
    out_shapes = (
        jax.ShapeDtypeStruct((M, D_ATTN), BF16),
        jax.ShapeDtypeStruct((B, n_keep, D_ATTN), F32),
        jax.ShapeDtypeStruct((B, n_keep, D_ATTN), F32),
        jax.ShapeDtypeStruct(kbuf.shape, BF16),
        jax.ShapeDtypeStruct(vbuf.shape, BF16),
        jax.ShapeDtypeStruct((M, SHIFT_COLS), F32),
        jax.ShapeDtypeStruct((M, D_MODEL), BF16),
        jax.ShapeDtypeStruct((M, D_MODEL), BF16),
    )
    tail = pl.BlockSpec((nb, tt, D_ATTN), lambda b, t: (b, 0, 0))
    behind = pl.BlockSpec((nb, tt, D_ATTN), lambda b, t: (b, t + pad_blocks, 0))
    return pl.pallas_call(
        functools.partial(_proj_in_kernel, nb=nb, tt=tt),
        grid=(B // nb, nt),
        in_specs=[
            pl.BlockSpec((tm, D_MODEL), row), modspec, modspec, _const_spec((1, D_MODEL)),
            _const_spec((D_MODEL, PROJ_COLS)),
            pl.BlockSpec(memory_space=pl.ANY), pl.BlockSpec(memory_space=pl.ANY),
        ],
        out_specs=(
            pl.BlockSpec((tm, D_ATTN), row), tail, tail, behind, behind,
            pl.BlockSpec((tm, SHIFT_COLS), row), pl.BlockSpec((tm, D_MODEL), row), pl.BlockSpec((tm, D_MODEL), row),
        ),
        out_shape=out_shapes,
        input_output_aliases={5: 3, 6: 4},
        compiler_params=_cparams("parallel", "arbitrary"),
        name="proj_in",
    )(x2, sh, sc, g, w_in_b, kbuf, vbuf)


def _attn_kernel(q_ref, k_ref, v_ref, bias_ref, o_ref, *, nbb, tq, n_sub, band, masked):
    j = pl.program_id(1)
    sls = [slice(h * HEAD_DIM, (h + 1) * HEAD_DIM) for h in range(N_HEADS)]
    qs, kbs, vbs, valids = [], [], [], []
    for bb in range(nbb):
        for i in range(n_sub):
            start = pl.multiple_of((j * n_sub + i) * tq, tq)
            r0 = (bb * n_sub + i) * tq
            qs.append(q_ref[r0:r0 + tq, :])
            kbs.append(k_ref[bb, pl.ds(start, band), :])
            vbs.append(v_ref[bb, pl.ds(start, band), :])
            valids.append(lax.broadcasted_iota(jnp.int32, (1, band), 1) >= BAND_LEFT - start)
    n = nbb * n_sub
    ss = [[_dot_nt(qs[i][:, sl], kbs[i][:, sl]) + bias_ref[h] for h, sl in enumerate(sls)] for i in range(n)]
    if masked:
        ss = [[jnp.where(valids[i], s, NEG_INF) for s in ss[i]] for i in range(n)]
    ps = [[jnp.exp2(s - jnp.max(s, axis=-1, keepdims=True)) for s in ss[i]] for i in range(n)]
    ls = [[jnp.sum(p, axis=-1, keepdims=True) for p in ps[i]] for i in range(n)]
    os_ = [[_dot(ps[i][h].astype(BF16), vbs[i][:, sl]) for h, sl in enumerate(sls)] for i in range(n)]
    for i in range(n):
        o = jnp.concatenate([os_[i][h] / ls[i][h] for h in range(N_HEADS)], axis=-1)
        o_ref[i * tq:(i + 1) * tq, :] = o.astype(BF16)


def _attention(q2, kbuf, vbuf, bias, *, B, T, nbb, tq, n_sub, masked):
    band = bias.shape[-1]
    Tp = kbuf.shape[1]
    nj = T // (tq * n_sub)
    assert nbb == 1 or nj == 1
    rows = nbb * tq * n_sub
    row = lambda b, j: (b * nj + j, 0)
    kv = pl.BlockSpec((nbb, Tp, D_ATTN), lambda b, j: (b, 0, 0))
    return pl.pallas_call(
        functools.partial(_attn_kernel, nbb=nbb, tq=tq, n_sub=n_sub, band=band, masked=masked),
        grid=(B // nbb, nj),
        in_specs=[pl.BlockSpec((rows, D_ATTN), row), kv, kv, _const_spec(bias.shape)],
        out_specs=pl.BlockSpec((rows, D_ATTN), row),
        out_shape=jax.ShapeDtypeStruct((B * T, D_ATTN), BF16),
        compiler_params=_cparams("parallel", "arbitrary"),
        name="band_attn",
    )(q2, kbuf, vbuf, bias)


def _split_bf16(x, n):
    parts = []
    for _ in range(n):
        p = x.astype(BF16)
        parts.append(p)
        x = x - p.astype(F32)
    return parts


def _rwkv_kernel(zs_ref, sh0_ref, s0_ref, mu_ref, w0_ref, wdu_ref, a0_ref, wiu_ref, wrg_ref, kk_ref, ka_ref,
                 rk_ref, gng_ref, gnb_ref, ones_ref, tri_ref,
                 ob_ref, sout_ref, carry_ref, st_ref, *, nbr, n_cr, C):
    tr = n_cr * C
    TT = nbr * tr

    @pl.when(pl.program_id(1) == 0)
    def _():
        carry_ref[...] = sh0_ref[...]
        st_ref[...] = s0_ref[...]

    zs3 = zs_ref[...]
    zs = zs3.reshape(TT, SHIFT_COLS)
    rows = lax.broadcasted_iota(jnp.int32, (TT, 1), 0)
    prev = jnp.where(rows % tr == 0, _expand_mod(carry_ref, nbr, tr), pltpu.roll(zs, 1, 0))
    carry_ref[...] = zs3[:, tr - 1:tr, :]
    zx = zs + (prev - zs) * mu_ref[...]

    r = zx[:, 0:D_RWKV]
    kx = zx[:, D_RWKV:2 * D_RWKV]
    v = zx[:, 2 * D_RWKV:3 * D_RWKV]
    o = 3 * D_RWKV
    wd = zx[:, o:o + DECAY_RANK]
    ad = zx[:, o + DECAY_RANK:o + DECAY_RANK + ICLR_RANK]
    gd = zx[:, o + DECAY_RANK + ICLR_RANK:]

    ones_bd = ones_ref[...]

    def segsum(x, n_terms):
        halves = []
        for gi in range(NG):
            acc = None
            for part in _split_bf16(x[:, gi * GW:(gi + 1) * GW], n_terms):
                d = _dot(part, ones_bd)
                acc = d if acc is None else acc + d
            halves.append(acc)
        return jnp.concatenate(halves, axis=-1)

    xw = w0_ref[...] + _dot(jnp.tanh(wd).astype(BF16), wdu_ref[...])
    w_log = -(jnp.maximum(-xw, 0.0) + jnp.log(1.0 + jnp.exp(-jnp.abs(xw)))) - 0.5
    lw = -jnp.exp(w_log)
    a = _sigmoid(a0_ref[...] + _dot(ad.astype(BF16), wiu_ref[...]))
    g = _dot(_sigmoid(gd).astype(BF16), wrg_ref[...])
    kkv = kx * kk_ref[...]
    kap = kkv * lax.rsqrt(jnp.maximum(segsum(kkv * kkv, SUM_TERMS), 1e-24))
    kh = kx * (1.0 + (a - 1.0) * ka_ref[...])
    bb = kap * a

    tri = tri_ref[...]
    cb = tri.shape[0]
    cums = []
    for r0 in range(0, TT, cb):
        acc = None
        for part in _split_bf16(lw[r0:r0 + cb], CUM_TERMS):
            d = _dot(tri, part)
            acc = d if acc is None else acc + d
        cums.append(acc)
    cum = cums[0] if len(cums) == 1 else jnp.concatenate(cums, axis=0)

    ecp = jnp.exp(cum - lw)
    kt_all = kap * ecp
    rt_all = r * (ecp * jnp.exp(lw))
    en = jnp.exp(-cum)
    kh_all = kh * en
    bh_all = bb * en

    n_ch = nbr * n_cr
    cw = GH * C
    n_sq = C.bit_length() - 2
    t_cat = lax.broadcasted_iota(jnp.int32, (C, cw), 0)
    l_cat = lax.broadcasted_iota(jnp.int32, (C, cw), 1)
    s_cat = l_cat - (l_cat // C) * C
    strict = t_cat > s_cat
    incl = t_cat >= s_cat
    eye_cat = (t_cat == s_cat).astype(F32)

    def head_mask(n_rows, rows_per_head, n_cols, cols_per_head):
        ri = lax.broadcasted_iota(jnp.int32, (n_rows, n_cols), 0) // rows_per_head
        ci = lax.broadcasted_iota(jnp.int32, (n_rows, n_cols), 1) // cols_per_head
        return ri == ci

    m_tk = head_mask(cw, C, GW, HEAD_DIM)
    m_tt = m_tk if C == HEAD_DIM else head_mask(cw, C, cw, C)
    m_kk = m_tk if C == HEAD_DIM else head_mask(GW, HEAD_DIM, GW, HEAD_DIM)
    diag_kk = (lax.broadcasted_iota(jnp.int32, (GW, GW), 0) == lax.broadcasted_iota(jnp.int32, (GW, GW), 1))

    def bdiag(x, mask):
        return jnp.where(mask, jnp.concatenate([x] * GH, axis=0), jnp.zeros((), x.dtype))

    ed_rows, pcs = [], []
    for c in range(n_ch):
        cum_end = cum[(c + 1) * C - 1:(c + 1) * C, :]
        ed_rows.append(jnp.exp(cum_end - cum[c * C:(c + 1) * C]))
        pcs.append(jnp.exp(cum_end))
    ed = ed_rows[0] if n_ch == 1 else jnp.concatenate(ed_rows, axis=0)
    kd_b = (kh * ed).astype(BF16)
    bd_b = (bb * ed).astype(BF16)
    v_b = v.astype(BF16)
    kt_b = kt_all.astype(BF16)
    rt_b = rt_all.astype(BF16)
    khb = kh_all.astype(BF16)
    bhb = bh_all.astype(BF16)

    probs = [(c, gi) for c in range(n_ch) for gi in range(NG)]
    rsl = lambda c: slice(c * C, (c + 1) * C)
    gsl = lambda gi: slice(gi * GW, (gi + 1) * GW)
    cut = lambda arr, pr: arr[rsl(pr[0]), gsl(pr[1])]
    xs = [jnp.concatenate([cut(kt_b, pr), cut(rt_b, pr)], axis=0) for pr in probs]
    xks = [_dot_nt(x, bdiag(cut(khb, pr), m_tk)) for x, pr in zip(xs, probs)]
    xbs = [_dot_nt(x, bdiag(cut(bhb, pr), m_tk)) for x, pr in zip(xs, probs)]
    vbds = [bdiag(cut(v_b, pr), m_tk) for pr in probs]
    mas = [jnp.concatenate([jnp.where(strict, xk[:C], 0.0), jnp.where(incl, xk[C:], 0.0)], axis=0).astype(BF16)
           for xk in xks]
    ps = [jnp.where(strict, xb[:C], 0.0) for xb in xbs]
    arbs = [jnp.where(incl, xb[C:], 0.0).astype(BF16) for xb in xbs]
    mavs = [_dot(ma, vbd) for ma, vbd in zip(mas, vbds)]
    tms = [eye_cat - p for p in ps]
    pbds = [bdiag(p.astype(BF16), m_tt) for p in ps]
    for _ in range(n_sq):
        ps = [_dot(p.astype(BF16), pbd) for p, pbd in zip(ps, pbds)]
        pbds = [bdiag(p.astype(BF16), m_tt) for p in ps]
        tms = [tm + _dot(tm.astype(BF16), pbd) for tm, pbd in zip(tms, pbds)]
    tmbs = [tm.astype(BF16) for tm in tms]
    tks = [_dot(tmb, bdiag(cut(kt_b, pr), m_tk)).astype(BF16) for tmb, pr in zip(tmbs, probs)]
    tmvs = [_dot(tmb, bdiag(mav[:C].astype(BF16), m_tk)).astype(BF16) for tmb, mav in zip(tmbs, mavs)]
    qps = [(cut(rt_all, pr) - _dot(arb, bdiag(tk, m_tk))).astype(BF16) for pr, arb, tk in zip(probs, arbs, tks)]
    y0s = [mav[C:] - _dot(arb, bdiag(tmv, m_tk)) for mav, arb, tmv in zip(mavs, arbs, tmvs)]
    gms, hms = [], []
    for i, pr in enumerate(probs):
        bdc = cut(bd_b, pr)
        gfull = jnp.where(diag_kk, pcs[pr[0]][:, gsl(pr[1])], 0.0) - _dot_tn(bdc, tks[i])
        gms.append(jnp.where(m_kk, gfull, 0.0).astype(BF16))
        hfull = _dot_tn(jnp.concatenate([cut(kd_b, pr), bdc], axis=0),
                        jnp.concatenate([cut(v_b, pr), -tmvs[i]], axis=0))
        hms.append(jnp.where(m_kk, hfull, 0.0))
    sts = [[st_ref[b, gi] for gi in range(NG)] for b in range(nbr)]
    ys = [[None] * NG for _ in range(n_ch)]
    for cc in range(n_cr):
        for b in range(nbr):
            for gi in range(NG):
                c = b * n_cr + cc
                i = c * NG + gi
                r2 = _dot(jnp.concatenate([qps[i], gms[i]], axis=0), sts[b][gi].astype(BF16))
                ys[c][gi] = r2[:C] + y0s[i]
                sts[b][gi] = r2[C:] + hms[i]
    for b in range(nbr):
        for gi in range(NG):
            st_ref[b, gi] = sts[b][gi]
            sout_ref[b, gi] = sts[b][gi]
    y = jnp.concatenate([jnp.concatenate(yc, axis=-1) for yc in ys], axis=0)

    inv_n = 1.0 / HEAD_DIM
    mean = segsum(y, SUM_TERMS) * inv_n
    dlt = y - mean
    var = segsum(dlt * dlt, SUM_TERMS) * inv_n
    yn = dlt * lax.rsqrt(var + GN_EPS) * gng_ref[...] + gnb_ref[...]
    yn = yn + segsum(r * kh * rk_ref[...], SUM_TERMS) * v
    ob_ref[...] = (yn * g).astype(BF16).reshape(ob_ref.shape)


def _rwkv(zs2, shift0, st0, mu, w0, wdu_b, a0, wiu_b, wrg_b, k_k, k_a, r_k, gn_g, gn_b, ones_bd, tri, *,
          B, T, nbr, n_cr, C):
    tr = n_cr * C
    TT = nbr * tr
    assert T % tr == 0 and B % nbr == 0
    vec = lambda n: _const_spec((1, n))
    row3 = lambda b, t: (b, t, 0)
    ob, st1 = pl.pallas_call(
        functools.partial(_rwkv_kernel, nbr=nbr, n_cr=n_cr, C=C),
        grid=(B // nbr, T // tr),
        in_specs=[
            pl.BlockSpec((nbr, tr, SHIFT_COLS), row3),
            pl.BlockSpec((nbr, 1, SHIFT_COLS), lambda b, t: (b, 0, 0)),
            pl.BlockSpec((nbr, NG, GW, GW), lambda b, t: (b, 0, 0, 0)),
            vec(SHIFT_COLS), vec(D_RWKV), _const_spec((DECAY_RANK, D_RWKV)), vec(D_RWKV),
            _const_spec((ICLR_RANK, D_RWKV)), _const_spec((GATE_RANK, D_RWKV)),
            vec(D_RWKV), vec(D_RWKV), vec(D_RWKV), vec(D_RWKV), vec(D_RWKV),
            _const_spec((GW, GW)), _const_spec(tri.shape),
        ],
        out_specs=(
            pl.BlockSpec((nbr, tr, D_RWKV), row3),
            pl.BlockSpec((nbr, NG, GW, GW), lambda b, t: (b, 0, 0, 0)),
        ),
        out_shape=(
            jax.ShapeDtypeStruct((B, T, D_RWKV), BF16),
            jax.ShapeDtypeStruct((B, NG, GW, GW), F32),
        ),
        scratch_shapes=[pltpu.VMEM((nbr, 1, SHIFT_COLS), F32), pltpu.VMEM((nbr, NG, GW, GW), F32)],
        compiler_params=_cparams("parallel", "arbitrary"),
        name="rwkv7",
    )(zs2.reshape(B, T, SHIFT_COLS), shift0, st0, mu, w0, wdu_b, a0, wiu_b, wrg_b, k_k, k_a, r_k, gn_g, gn_b,
      ones_bd, tri)
    return ob.reshape(B * T, D_RWKV), st1


def _merge_kernel(x_ref, oa_ref, ob_ref, ga_ref, gb_ref, gt_ref, woa_ref, wob_ref, wo_ref, o_ref, *, nb, tt):
    ma = ga_ref[...].astype(F32) * _dot(oa_ref[...], woa_ref[...])
    mb = gb_ref[...].astype(F32) * _dot(ob_ref[...], wob_ref[...])
    merged = (ma + mb).astype(BF16)
    o_ref[...] = x_ref[...] + _expand_mod(gt_ref, nb, tt) * _dot(merged, wo_ref[...])


def _merge(x2, oa, ob, ga, gb, gt, woa_b, wob_b, wo_b, *, B, T, nb, tt):
    M = B * T
    tm = nb * tt
    nt = T // tt
    row = lambda b, t: (b * nt + t, 0)
    act = lambda n: pl.BlockSpec((tm, n), row)
    return pl.pallas_call(
        functools.partial(_merge_kernel, nb=nb, tt=tt),
        grid=(B // nb, nt),
        in_specs=[
            act(D_MODEL), act(D_ATTN), act(D_RWKV), act(D_MODEL), act(D_MODEL),
            pl.BlockSpec((nb, 1, D_MODEL), lambda b, t: (b, 0, 0)),
            _const_spec((D_ATTN, D_MODEL)), _const_spec((D_RWKV, D_MODEL)), _const_spec((D_MODEL, D_MODEL)),
        ],
        out_specs=act(D_MODEL),
        out_shape=jax.ShapeDtypeStruct((M, D_MODEL), F32),
        compiler_params=_cparams("parallel", "parallel"),
        name="merge_out",
    )(x2, oa, ob, ga, gb, gt, woa_b, wob_b, wo_b)


_FF_STEP = D_FF // 2


def _ffn_kernel(x_ref, sh_ref, sc_ref, gt_ref, g_ref, gf_ref, wi_ref, wo_ref, o_ref, *, nb, tt, final):
    x = x_ref[...]
    h = _rms(x, g_ref[...]) * (1.0 + _expand_mod(sc_ref, nb, tt)) + _expand_mod(sh_ref, nb, tt)
    hb = h.astype(BF16)
    acc = None
    for a in range(0, D_FF, _FF_STEP):
        u = _dot(hb, wi_ref[:, a:a + _FF_STEP])
        gg = _dot(hb, wi_ref[:, D_FF + a:D_FF + a + _FF_STEP])
        act = (gg * _sigmoid(gg) * u).astype(BF16)
        d = _dot(act, wo_ref[a:a + _FF_STEP, :])
        acc = d if acc is None else acc + d
    y = x + _expand_mod(gt_ref, nb, tt) * acc
    if final:
        y = _rms(y, gf_ref[...])
    o_ref[...] = y


def _ffn(x2, sh, sc, gt, g, g_final, wi_b, wo_b, *, B, T, nb, tt, final):
    M = B * T
    tm = nb * tt
    nt = T // tt
    row = lambda b, t: (b * nt + t, 0)
    modspec = pl.BlockSpec((nb, 1, D_MODEL), lambda b, t: (b, 0, 0))
    return pl.pallas_call(
        functools.partial(_ffn_kernel, nb=nb, tt=tt, final=final),
        grid=(B // nb, nt),
        in_specs=[
            pl.BlockSpec((tm, D_MODEL), row), modspec, modspec, modspec,
            _const_spec((1, D_MODEL)), _const_spec((1, D_MODEL)),
            _const_spec((D_MODEL, 2 * D_FF)), _const_spec((D_FF, D_MODEL)),
        ],
        out_specs=pl.BlockSpec((tm, D_MODEL), row),
        out_shape=jax.ShapeDtypeStruct((M, D_MODEL), F32),
        compiler_params=_cparams("parallel", "parallel"),
        name="ffn",
    )(x2, sh, sc, gt, g, g_final, wi_b, wo_b)


def _bias_kernel(rb_ref, o_ref, *, tq, band):
    n_rel = rb_ref.shape[1]
    parts = _split_bf16(rb_ref[...], 3)
    col = lax.broadcasted_iota(jnp.int32, (n_rel, band), 1)
    bucket = lax.broadcasted_iota(jnp.int32, (n_rel, band), 0)
    for q in range(tq):
        idx = jnp.clip(col - (BAND_LEFT + q), -REL_CLIP, CHUNK - 1) + REL_CLIP
        onehot = jnp.where(idx == bucket, 1.0, 0.0).astype(BF16)
        o_ref[q] = (_dot(parts[0], onehot) + _dot(parts[1], onehot)) + _dot(parts[2], onehot)


def _bias_tables(rel_bias, tq, band):
    depth, n_heads, n_rel = rel_bias.shape
    r = depth * n_heads
    t = pl.pallas_call(
        functools.partial(_bias_kernel, tq=tq, band=band),
        out_shape=jax.ShapeDtypeStruct((tq, r, band), F32),
        compiler_params=pltpu.CompilerParams(vmem_limit_bytes=VMEM_LIMIT_BYTES),
        name="bias_table",
    )(rel_bias.reshape(r, n_rel))
    return jnp.transpose(t, (1, 0, 2)).reshape(depth, n_heads, tq, band)


def _group_bias(tab, tq):
    tab = tab * LOG2E
    if tq <= CHUNK:
        return tab[:, :tq, :BAND_LEFT + tq]
    n = tq // CHUNK
    rows = [jnp.pad(tab, ((0, 0), (0, 0), (i * CHUNK, (n - 1 - i) * CHUNK)), constant_values=NEG_INF)
            for i in range(n)]
    return jnp.concatenate(rows, axis=1)


def _tile_consts(TT, C):
    cb = min(TT, CUM_BLOCK)
    assert TT % cb == 0 and cb % C == 0
    i = jnp.arange(cb)
    tri = ((i[:, None] >= i[None, :]) & (i[:, None] // C == i[None, :] // C)).astype(BF16)
    j = jnp.arange(GW)
    ones_bd = (j[:, None] // HEAD_DIM == j[None, :] // HEAD_DIM).astype(BF16)
    return ones_bd, tri


def _layer(x2, mods, kbuf, vbuf, shift0, st0, lw, g_final, *, B, T, nb, tt, nbb, tq, n_sub, masked, nbr, n_cr, C,
           final):
    sh1, sc1, gt1, sh2, sc2, gt2 = mods
    q, k_tail, v_tail, kbuf, vbuf, zs, ga, gb = _proj_in(
        x2, sh1, sc1, lw['g1'], lw['w_in'], kbuf, vbuf, B=B, T=T, nb=nb, tt=tt)
    oa = _attention(q, kbuf, vbuf, lw['bias'], B=B, T=T, nbb=nbb, tq=tq, n_sub=n_sub, masked=masked)
    ob, st1 = _rwkv(zs, shift0, st0, lw['mu'], lw['w0'], lw['wdu'], lw['a0'], lw['wiu'], lw['wrg'], lw['k_k'],
                    lw['k_a'], lw['r_k'], lw['gn_g'], lw['gn_b'], lw['ones_bd'], lw['tri'], B=B, T=T, nbr=nbr, n_cr=n_cr, C=C)
    x2 = _merge(x2, oa, ob, ga, gb, gt1, lw['woa'], lw['wob'], lw['wo'], B=B, T=T, nb=nb, tt=tt)
    x2 = _ffn(x2, sh2, sc2, gt2, lw['g2'], g_final, lw['wfi'], lw['wfo'], B=B, T=T, nb=nb, tt=tt, final=final)
    return x2, k_tail, v_tail, kbuf, vbuf, st1, zs


def _path_config(B, T):
    n_keep = min(BAND_LEFT, T)
    tt = n_keep
    nb = 1 if T > tt else min(B, max(1, 256 // tt))
    if T >= CHUNK:
        tq = CHUNK * min(ATTN_GROUP_CHUNKS, T // CHUNK)
        nbb, n_sub, masked = 1, max(1, min(ATTN_STEP_ROWS, T) // tq), True
    else:
        nbb, tq, n_sub, masked = min(B, 8), T, 1, False
    C = min(CHUNK, T)
    n_cr = min(T // C, RWKV_ROW_CHUNKS)
    nbr = min(B, RWKV_MAX_BATCH_ROWS, max(1, RWKV_TILE_ROWS // (n_cr * C)))
    return dict(nb=nb, tt=tt, nbb=nbb, tq=tq, n_sub=n_sub, masked=masked, nbr=nbr, n_cr=n_cr, C=C)


def _layer_weights(l, p, cfgs, bias_all):
    row = lambda a: a.reshape(1, -1)
    lw = dict(
        g1=row(p['g_norm'][l, 0]), g2=row(p['g_norm'][l, 1]),
        w_in=p['w_in'][l].astype(BF16),
        mu=row(p['mu'][l]), w0=row(p['w0'][l]), wdu=p['w_decay_up'][l].astype(BF16), a0=row(p['a0'][l]),
        wiu=p['w_iclr_up'][l].astype(BF16), wrg=p['w_rg_up'][l].astype(BF16),
        k_k=row(p['k_k'][l]), k_a=row(p['k_a'][l]), r_k=row(p['r_k'][l]),
        gn_g=row(p['gn_g'][l]), gn_b=row(p['gn_b'][l]),
        woa=p['w_out_attn'][l].astype(BF16), wob=p['w_out_rwkv'][l].astype(BF16), wo=p['w_out'][l].astype(BF16),
        wfi=p['w_ffn_in'][l].astype(BF16), wfo=p['w_ffn_out'][l].astype(BF16),
    )
    out = []
    for cfg in cfgs:
        d = dict(lw)
        d['bias'] = _group_bias(bias_all[l], cfg['tq'])
        d['ones_bd'], d['tri'] = _tile_consts(cfg['nbr'] * cfg['n_cr'] * cfg['C'], cfg['C'])
        out.append(d)
    return out


def _state_to_groups(s):
    b = s.shape[0]
    x = jnp.swapaxes(s, -1, -2).reshape(b, NG, GH, HEAD_DIM, HEAD_DIM)
    eye = jnp.eye(GH, dtype=s.dtype)
    return jnp.einsum('bgjkv,jJ->bgjkJv', x, eye).reshape(b, NG, GW, GW)


def _state_from_groups(st):
    b = st.shape[0]
    x = st.reshape(b, NG, GH, HEAD_DIM, GH, HEAD_DIM)
    d = jnp.stack([x[:, :, j, :, j, :] for j in range(GH)], axis=2)
    return jnp.swapaxes(d.reshape(b, N_HEADS, HEAD_DIM, HEAD_DIM), -1, -2)


def _split_mods(mod, lo, hi):
    m = mod[lo:hi].reshape(hi - lo, 1, 6, D_MODEL)
    return tuple(m[:, :, i, :] for i in range(6))


def kernel(x_prompt, x_sample, cache_attn_k, cache_attn_v, state_wkv, state_shift, c_prompt, c_sample, w_ada, b_ada, g_norm, w_in, rel_bias, mu, w0, w_decay_up, a0, w_iclr_up, w_rg_up, k_k, k_a, r_k, gn_g, gn_b, w_out_attn, w_out_rwkv, w_out, w_ffn_in, w_ffn_out, g_final):
    p = dict(g_norm=g_norm, w_in=w_in, rel_bias=rel_bias, mu=mu, w0=w0, w_decay_up=w_decay_up, a0=a0,
             w_iclr_up=w_iclr_up, w_rg_up=w_rg_up, k_k=k_k, k_a=k_a, r_k=r_k, gn_g=gn_g, gn_b=gn_b,
             w_out_attn=w_out_attn, w_out_rwkv=w_out_rwkv, w_out=w_out, w_ffn_in=w_ffn_in, w_ffn_out=w_ffn_out)
    Bp, T, _ = x_prompt.shape
    Bs, Ts, _ = x_sample.shape
    W = cache_attn_k.shape[2]
    assert W == BAND_LEFT
    cfg_p, cfg_s = _path_config(Bp, T), _path_config(Bs, Ts)
    gf = g_final.reshape(1, D_MODEL)

    mod = _modulation(jnp.concatenate([c_prompt, c_sample], axis=0), w_ada, b_ada)
    bias_all = _bias_tables(rel_bias, CHUNK, BAND)

    xp = x_prompt.reshape(Bp * T, D_MODEL)
    xs = x_sample.reshape(Bs * Ts, D_MODEL)
    kbuf_p = jnp.zeros((Bp, BAND_LEFT + T, D_ATTN), BF16)
    vbuf_p = jnp.zeros((Bp, BAND_LEFT + T, D_ATTN), BF16)
    zero_shift = jnp.zeros((Bp, 1, SHIFT_COLS), F32)
    zero_state = jnp.zeros((Bp, NG, GW, GW), F32)
    n_keep = min(BAND_LEFT, T)

    outs = [[] for _ in range(8)]
    for l in range(DEPTH):
        lw_p, lw_s = _layer_weights(l, p, (cfg_p, cfg_s), bias_all)
        final = l == DEPTH - 1
        xp, k_t, v_t, kbuf_p, vbuf_p, st, zs = _layer(
            xp, _split_mods(mod[l], 0, Bp), kbuf_p, vbuf_p, zero_shift, zero_state, lw_p, gf,
            B=Bp, T=T, final=final, **cfg_p)
        outs[0].append(k_t.reshape(Bp, n_keep, N_HEADS, HEAD_DIM))
        outs[1].append(v_t.reshape(Bp, n_keep, N_HEADS, HEAD_DIM))
        outs[2].append(_state_from_groups(st))
        outs[3].append(zs.reshape(Bp, T, SHIFT_COLS)[:, -1])

        pad = ((0, 0), (0, Ts), (0, 0))
        kbuf_s = jnp.pad(cache_attn_k[l].reshape(Bs, W, D_ATTN).astype(BF16), pad)
        vbuf_s = jnp.pad(cache_attn_v[l].reshape(Bs, W, D_ATTN).astype(BF16), pad)
        xs, k_t, v_t, _, _, st, zs = _layer(
            xs, _split_mods(mod[l], Bp, Bp + Bs), kbuf_s, vbuf_s, state_shift[l].reshape(Bs, 1, SHIFT_COLS),
            _state_to_groups(state_wkv[l]), lw_s, gf, B=Bs, T=Ts, final=final, **cfg_s)
        outs[4].append(k_t.reshape(Bs, Ts, N_HEADS, HEAD_DIM))
        outs[5].append(v_t.reshape(Bs, Ts, N_HEADS, HEAD_DIM))
        outs[6].append(_state_from_groups(st))
        outs[7].append(zs.reshape(Bs, Ts, SHIFT_COLS)[:, -1])

    y_prompt = xp.reshape(Bp, T, D_MODEL)
    y_sample = xs.reshape(Bs, Ts, D_MODEL)
    return (y_prompt, y_sample) + tuple(jnp.stack(o) for o in outs)
```

```python
import functools

import jax
import jax.numpy as jnp
from jax import lax
from jax.experimental import pallas as pl
from jax.experimental.pallas import tpu as pltpu

D_MODEL = 1024
DEPTH = 4
CHUNK = 64
BAND_LEFT = 512
BAND = BAND_LEFT + CHUNK
N_HEADS = 8
HEAD_DIM = 64
D_ATTN = 512
D_RWKV = 512
REL_CLIP = 128
DECAY_RANK = 64
ICLR_RANK = 64
GATE_RANK = 128
SHIFT_COLS = 3 * D_RWKV + DECAY_RANK + ICLR_RANK + GATE_RANK
PROJ_COLS = 3 * D_ATTN + SHIFT_COLS + 2 * D_MODEL
D_FF = 2816
RMS_EPS = 1e-6
GN_EPS = 64e-5
NEG_INF = -1e30
LOG2E = 1.4426950408889634

F32 = jnp.float32
BF16 = jnp.bfloat16

RWKV_TILE_ROWS = 512
RWKV_ROW_CHUNKS = 4
RWKV_MAX_BATCH_ROWS = 8
ATTN_GROUP_CHUNKS = 2
ATTN_STEP_ROWS = 256
SUM_TERMS = 1
CUM_TERMS = 2
CUM_BLOCK = 256

GH = 4
GW = GH * HEAD_DIM
NG = N_HEADS // GH

VMEM_LIMIT_BYTES = 56 * 1024 * 1024


def _cparams(*sem):
    return pltpu.CompilerParams(dimension_semantics=sem, vmem_limit_bytes=VMEM_LIMIT_BYTES)


def _const_spec(shape):
    nd = len(shape)
    return pl.BlockSpec(shape, lambda *_: (0,) * nd, pipeline_mode=pl.Buffered(1))


def _dot(a, b):
    return jnp.dot(a, b, preferred_element_type=F32)


def _dot_nt(a, b):
    return lax.dot_general(a, b, (((1,), (1,)), ((), ())), preferred_element_type=F32)


def _dot_tn(a, b):
    return lax.dot_general(a, b, (((0,), (0,)), ((), ())), preferred_element_type=F32)


def _sigmoid(x):
    return 1.0 / (1.0 + jnp.exp(-x))


def _expand_mod(ref, nb, tt):
    m = ref[...]
    if nb == 1:
        return m[0]
    return jnp.broadcast_to(m, (nb, tt, m.shape[-1])).reshape(nb * tt, m.shape[-1])


def _rms(x, g):
    ms = jnp.mean(x * x, axis=-1, keepdims=True)
    return x * lax.rsqrt(ms + RMS_EPS) * g


def _mod_kernel(c_ref, w_ref, b_ref, o_ref):
    c = c_ref[...]
    s = (c * _sigmoid(c)).astype(BF16)
    o_ref[0] = _dot(s, w_ref[0].astype(BF16)) + b_ref[0]


def _modulation(c_all, w_ada, b_ada):
    nb = c_all.shape[0]
    tn = 1536
    return pl.pallas_call(
        _mod_kernel,
        grid=(DEPTH, 6 * D_MODEL // tn),
        in_specs=[
            pl.BlockSpec((nb, D_MODEL), lambda l, j: (0, 0)),
            pl.BlockSpec((1, D_MODEL, tn), lambda l, j: (l, 0, j)),
            pl.BlockSpec((1, 1, tn), lambda l, j: (l, 0, j)),
        ],
        out_specs=pl.BlockSpec((1, nb, tn), lambda l, j: (l, 0, j)),
        out_shape=jax.ShapeDtypeStruct((DEPTH, nb, 6 * D_MODEL), F32),
        compiler_params=_cparams("parallel", "parallel"),
        name="adaln_mod",
    )(c_all, w_ada, b_ada.reshape(DEPTH, 1, 6 * D_MODEL))


_Q0, _K0, _V0, _ZS0, _GA0, _GB0 = 0, 512, 1024, 1536, 3328, 4352
_COL_STEP = 512


def _proj_in_kernel(x_ref, sh_ref, sc_ref, g_ref, w_ref, kin_ref, vin_ref,
                    q_ref, k_ref, v_ref, kb_ref, vb_ref, zs_ref, ga_ref, gb_ref, *, nb, tt):
    del kin_ref, vin_ref
    x = x_ref[...]
    h = _rms(x, g_ref[...]) * (1.0 + _expand_mod(sc_ref, nb, tt)) + _expand_mod(sh_ref, nb, tt)
    hb = h.astype(BF16)

    def cols(a, b):
        return _dot(hb, w_ref[:, a:b])

    q_ref[...] = (cols(_Q0, _K0) * (HEAD_DIM ** -0.5 * LOG2E)).astype(BF16)
    k = cols(_K0, _V0)
    k_ref[...] = k.reshape(k_ref.shape)
    kb_ref[...] = k.astype(BF16).reshape(kb_ref.shape)
    v = cols(_V0, _ZS0)
    v_ref[...] = v.reshape(v_ref.shape)
    vb_ref[...] = v.astype(BF16).reshape(vb_ref.shape)
    for a in range(_ZS0, _GA0, _COL_STEP):
        b = min(a + _COL_STEP, _GA0)
        zs_ref[:, a - _ZS0:b - _ZS0] = cols(a, b)
    for a in range(_GA0, _GB0, _COL_STEP):
        ga_ref[:, a - _GA0:a - _GA0 + _COL_STEP] = _sigmoid(cols(a, a + _COL_STEP)).astype(BF16)
    for a in range(_GB0, PROJ_COLS, _COL_STEP):
        gb_ref[:, a - _GB0:a - _GB0 + _COL_STEP] = _sigmoid(cols(a, a + _COL_STEP)).astype(BF16)


_SH1, _SC1, _GT1, _SH2, _SC2, _GT2 = range(6)


def _mod_spec(nb, mrow, which):
    return pl.BlockSpec((nb, 1, D_MODEL), lambda b, t: (mrow + b, 0, which))


def _proj_in(x2, mod3, mrow, g, w_in_b, kbuf, vbuf, *, B, T, nb, tt):
    M = B * T
    tm = nb * tt
    n_keep = min(BAND_LEFT, T)
    assert tt == n_keep and T % tt == 0 and B % nb == 0 and BAND_LEFT % tt == 0
    assert nb == 1 or tt == T
    nt = T // tt
    pad_blocks = BAND_LEFT // tt
    row = lambda b, t: (b * nt + t, 0)
    out_shapes = (
        jax.ShapeDtypeStruct((M, D_ATTN), BF16),
        jax.ShapeDtypeStruct((B, n_keep, D_ATTN), F32),
        jax.ShapeDtypeStruct((B, n_keep, D_ATTN), F32),
        jax.ShapeDtypeStruct(kbuf.shape, BF16),
        jax.ShapeDtypeStruct(vbuf.shape, BF16),
        jax.ShapeDtypeStruct((M, SHIFT_COLS), F32),
        jax.ShapeDtypeStruct((M, D_MODEL), BF16),
        jax.ShapeDtypeStruct((M, D_MODEL), BF16),
    )
    tail = pl.BlockSpec((nb, tt, D_ATTN), lambda b, t: (b, 0, 0))
    behind = pl.BlockSpec((nb, tt, D_ATTN), lambda b, t: (b, t + pad_blocks, 0))
    return pl.pallas_call(
        functools.partial(_proj_in_kernel, nb=nb, tt=tt),
        grid=(B // nb, nt),
        in_specs=[
            pl.BlockSpec((tm, D_MODEL), row), _mod_spec(nb, mrow, _SH1), _mod_spec(nb, mrow, _SC1),
            _const_spec((1, D_MODEL)), _const_spec((D_MODEL, PROJ_COLS)),
            pl.BlockSpec(memory_space=pl.ANY), pl.BlockSpec(memory_space=pl.ANY),
        ],
        out_specs=(
            pl.BlockSpec((tm, D_ATTN), row), tail, tail, behind, behind,
            pl.BlockSpec((tm, SHIFT_COLS), row), pl.BlockSpec((tm, D_MODEL), row), pl.BlockSpec((tm, D_MODEL), row),
        ),
        out_shape=out_shapes,
        input_output_aliases={5: 3, 6: 4},
        compiler_params=_cparams("parallel", "arbitrary"),
        name="proj_in",
    )(x2, mod3, mod3, g, w_in_b, kbuf, vbuf)


def _attn_kernel(q_ref, k_ref, v_ref, bias_ref, o_ref, *, nbb, tq, n_sub, band, masked):
    j = pl.program_id(1)
    sls = [slice(h * HEAD_DIM, (h + 1) * HEAD_DIM) for h in range(N_HEADS)]
    qs, kbs, vbs, valids = [], [], [], []
    for bb in range(nbb):
        for i in range(n_sub):
            start = pl.multiple_of((j * n_sub + i) * tq, tq)
            r0 = (bb * n_sub + i) * tq
            qs.append(q_ref[r0:r0 + tq, :])
            kbs.append(k_ref[bb, pl.ds(start, band), :])
            vbs.append(v_ref[bb, pl.ds(start, band), :])
            valids.append(lax.broadcasted_iota(jnp.int32, (1, band), 1) >= BAND_LEFT - start)
    n = nbb * n_sub
    ss = [[_dot_nt(qs[i][:, sl], kbs[i][:, sl]) + bias_ref[h] for h, sl in enumerate(sls)] for i in range(n)]
    if masked:
        ss = [[jnp.where(valids[i], s, NEG_INF) for s in ss[i]] for i in range(n)]
    ps = [[jnp.exp2(s - jnp.max(s, axis=-1, keepdims=True)) for s in ss[i]] for i in range(n)]
    ls = [[jnp.sum(p, axis=-1, keepdims=True) for p in ps[i]] for i in range(n)]
    os_ = [[_dot(ps[i][h].astype(BF16), vbs[i][:, sl]) for h, sl in enumerate(sls)] for i in range(n)]
    for i in range(n):
        o = jnp.concatenate([os_[i][h] / ls[i][h] for h in range(N_HEADS)], axis=-1)
        o_ref[i * tq:(i + 1) * tq, :] = o.astype(BF16)


def _attention(q2, kbuf, vbuf, bias, *, B, T, nbb, tq, n_sub, masked):
    band = bias.shape[-1]
    Tp = kbuf.shape[1]
    nj = T // (tq * n_sub)
    assert nbb == 1 or nj == 1
    rows = nbb * tq * n_sub
    row = lambda b, j: (b * nj + j, 0)
    kv = pl.BlockSpec((nbb, Tp, D_ATTN), lambda b, j: (b, 0, 0))
    return pl.pallas_call(
        functools.partial(_attn_kernel, nbb=nbb, tq=tq, n_sub=n_sub, band=band, masked=masked),
        grid=(B // nbb, nj),
        in_specs=[pl.BlockSpec((rows, D_ATTN), row), kv, kv, _const_spec(bias.shape)],
        out_specs=pl.BlockSpec((rows, D_ATTN), row),
        out_shape=jax.ShapeDtypeStruct((B * T, D_ATTN), BF16),
        compiler_params=_cparams("parallel", "arbitrary"),
        name="band_attn",
    )(q2, kbuf, vbuf, bias)


def _split_bf16(x, n):
    parts = []
    for _ in range(n):
        p = x.astype(BF16)
        parts.append(p)
        x = x - p.astype(F32)
    return parts


def _rwkv_kernel(zs_ref, sh0_ref, s0_ref, mu_ref, w0_ref, wdu_ref, a0_ref, wiu_ref, wrg_ref, kk_ref, ka_ref,
                 rk_ref, gng_ref, gnb_ref, ones_ref, tri_ref,
                 ob_ref, sout_ref, carry_ref, st_ref, *, nbr, n_cr, C):
    tr = n_cr * C
    TT = nbr * tr

    @pl.when(pl.program_id(1) == 0)
    def _():
        carry_ref[...] = sh0_ref[...]
        st_ref[...] = s0_ref[...]

    zs3 = zs_ref[...]
    zs = zs3.reshape(TT, SHIFT_COLS)
    rows = lax.broadcasted_iota(jnp.int32, (TT, 1), 0)
    prev = jnp.where(rows % tr == 0, _expand_mod(carry_ref, nbr, tr), pltpu.roll(zs, 1, 0))
    carry_ref[...] = zs3[:, tr - 1:tr, :]
    zx = zs + (prev - zs) * mu_ref[...]

    r = zx[:, 0:D_RWKV]
    kx = zx[:, D_RWKV:2 * D_RWKV]
    v = zx[:, 2 * D_RWKV:3 * D_RWKV]
    o = 3 * D_RWKV
    wd = zx[:, o:o + DECAY_RANK]
    ad = zx[:, o + DECAY_RANK:o + DECAY_RANK + ICLR_RANK]
    gd = zx[:, o + DECAY_RANK + ICLR_RANK:]

    ones_bd = ones_ref[...]

    def segsum(x, n_terms):
        halves = []
        for gi in range(NG):
            acc = None
            for part in _split_bf16(x[:, gi * GW:(gi + 1) * GW], n_terms):
                d = _dot(part, ones_bd)
                acc = d if acc is None else acc + d
            halves.append(acc)
        return jnp.concatenate(halves, axis=-1)

    xw = w0_ref[...] + _dot(jnp.tanh(wd).astype(BF16), wdu_ref[...])
    w_log = -(jnp.maximum(-xw, 0.0) + jnp.log(1.0 + jnp.exp(-jnp.abs(xw)))) - 0.5
    lw = -jnp.exp(w_log)
    a = _sigmoid(a0_ref[...] + _dot(ad.astype(BF16), wiu_ref[...]))
    g = _dot(_sigmoid(gd).astype(BF16), wrg_ref[...])
    kkv = kx * kk_ref[...]
    kap = kkv * lax.rsqrt(jnp.maximum(segsum(kkv * kkv, SUM_TERMS), 1e-24))
    kh = kx * (1.0 + (a - 1.0) * ka_ref[...])
    bb = kap * a

    tri = tri_ref[...]
    cb = tri.shape[0]
    cums = []
    for r0 in range(0, TT, cb):
        acc = None
        for part in _split_bf16(lw[r0:r0 + cb], CUM_TERMS):
            d = _dot(tri, part)
            acc = d if acc is None else acc + d
        cums.append(acc)
    cum = cums[0] if len(cums) == 1 else jnp.concatenate(cums, axis=0)

    ecp = jnp.exp(cum - lw)
    kt_all = kap * ecp
    rt_all = r * (ecp * jnp.exp(lw))
    en = jnp.exp(-cum)
    kh_all = kh * en
    bh_all = bb * en

    n_ch = nbr * n_cr
    cw = GH * C
    n_sq = C.bit_length() - 2
    t_cat = lax.broadcasted_iota(jnp.int32, (C, cw), 0)
    l_cat = lax.broadcasted_iota(jnp.int32, (C, cw), 1)
    s_cat = l_cat - (l_cat // C) * C
    strict = t_cat > s_cat
    incl = t_cat >= s_cat
    eye_cat = (t_cat == s_cat).astype(F32)

    def head_mask(n_rows, rows_per_head, n_cols, cols_per_head):
        ri = lax.broadcasted_iota(jnp.int32, (n_rows, n_cols), 0) // rows_per_head
        ci = lax.broadcasted_iota(jnp.int32, (n_rows, n_cols), 1) // cols_per_head
        return ri == ci

    m_tk = head_mask(cw, C, GW, HEAD_DIM)
    m_tt = m_tk if C == HEAD_DIM else head_mask(cw, C, cw, C)
    m_kk = m_tk if C == HEAD_DIM else head_mask(GW, HEAD_DIM, GW, HEAD_DIM)
    diag_kk = (lax.broadcasted_iota(jnp.int32, (GW, GW), 0) == lax.broadcasted_iota(jnp.int32, (GW, GW), 1))

    def bdiag(x, mask):
        return jnp.where(mask, jnp.concatenate([x] * GH, axis=0), jnp.zeros((), x.dtype))

    ed_rows, pcs = [], []
    for c in range(n_ch):
        cum_end = cum[(c + 1) * C - 1:(c + 1) * C, :]
        ed_rows.append(jnp.exp(cum_end - cum[c * C:(c + 1) * C]))
        pcs.append(jnp.exp(cum_end))
    ed = ed_rows[0] if n_ch == 1 else jnp.concatenate(ed_rows, axis=0)
    kd_b = (kh * ed).astype(BF16)
    bd_b = (bb * ed).astype(BF16)
    v_b = v.astype(BF16)
    kt_b = kt_all.astype(BF16)
    rt_b = rt_all.astype(BF16)
    khb = kh_all.astype(BF16)
    bhb = bh_all.astype(BF16)

    probs = [(c, gi) for c in range(n_ch) for gi in range(NG)]
    rsl = lambda c: slice(c * C, (c + 1) * C)
    gsl = lambda gi: slice(gi * GW, (gi + 1) * GW)
    cut = lambda arr, pr: arr[rsl(pr[0]), gsl(pr[1])]
    xs = [jnp.concatenate([cut(kt_b, pr), cut(rt_b, pr)], axis=0) for pr in probs]
    xks = [_dot_nt(x, bdiag(cut(khb, pr), m_tk)) for x, pr in zip(xs, probs)]
    xbs = [_dot_nt(x, bdiag(cut(bhb, pr), m_tk)) for x, pr in zip(xs, probs)]
    vbds = [bdiag(cut(v_b, pr), m_tk) for pr in probs]
    mas = [jnp.concatenate([jnp.where(strict, xk[:C], 0.0), jnp.where(incl, xk[C:], 0.0)], axis=0).astype(BF16)
           for xk in xks]
    ps = [jnp.where(strict, xb[:C], 0.0) for xb in xbs]
    arbs = [jnp.where(incl, xb[C:], 0.0).astype(BF16) for xb in xbs]
    mavs = [_dot(ma, vbd) for ma, vbd in zip(mas, vbds)]
    tms = [eye_cat - p for p in ps]
    pbds = [bdiag(p.astype(BF16), m_tt) for p in ps]
    for _ in range(n_sq):
        ps = [_dot(p.astype(BF16), pbd) for p, pbd in zip(ps, pbds)]
        pbds = [bdiag(p.astype(BF16), m_tt) for p in ps]
        tms = [tm + _dot(tm.astype(BF16), pbd) for tm, pbd in zip(tms, pbds)]
    tmbs = [tm.astype(BF16) for tm in tms]
    tks = [_dot(tmb, bdiag(cut(kt_b, pr), m_tk)).astype(BF16) for tmb, pr in zip(tmbs, probs)]
    tmvs = [_dot(tmb, bdiag(mav[:C].astype(BF16), m_tk)).astype(BF16) for tmb, mav in zip(tmbs, mavs)]
    qps = [(cut(rt_all, pr) - _dot(arb, bdiag(tk, m_tk))).astype(BF16) for pr, arb, tk in zip(probs, arbs, tks)]
    y0s = [mav[C:] - _dot(arb, bdiag(tmv, m_tk)) for mav, arb, tmv in zip(mavs, arbs, tmvs)]
    gms, hms = [], []
    for i, pr in enumerate(probs):
        bdc = cut(bd_b, pr)
        gfull = jnp.where(diag_kk, pcs[pr[0]][:, gsl(pr[1])], 0.0) - _dot_tn(bdc, tks[i])
        gms.append(jnp.where(m_kk, gfull, 0.0).astype(BF16))
        hfull = _dot_tn(jnp.concatenate([cut(kd_b, pr), bdc], axis=0),
                        jnp.concatenate([cut(v_b, pr), -tmvs[i]], axis=0))
        hms.append(jnp.where(m_kk, hfull, 0.0))
    sts = [[st_ref[b, gi] for gi in range(NG)] for b in range(nbr)]
    ys = [[None] * NG for _ in range(n_ch)]
    for cc in range(n_cr):
        for b in range(nbr):
            for gi in range(NG):
                c = b * n_cr + cc
                i = c * NG + gi
                r2 = _dot(jnp.concatenate([qps[i], gms[i]], axis=0), sts[b][gi].astype(BF16))
                ys[c][gi] = r2[:C] + y0s[i]
                sts[b][gi] = r2[C:] + hms[i]
    for b in range(nbr):
        for gi in range(NG):
            st_ref[b, gi] = sts[b][gi]
            sout_ref[b, gi] = sts[b][gi]
    y = jnp.concatenate([jnp.concatenate(yc, axis=-1) for yc in ys], axis=0)

    inv_n = 1.0 / HEAD_DIM
    mean = segsum(y, SUM_TERMS) * inv_n
    dlt = y - mean
    var = segsum(dlt * dlt, SUM_TERMS) * inv_n
    yn = dlt * lax.rsqrt(var + GN_EPS) * gng_ref[...] + gnb_ref[...]
    yn = yn + segsum(r * kh * rk_ref[...], SUM_TERMS) * v
    ob_ref[...] = (yn * g).astype(BF16).reshape(ob_ref.shape)


def _rwkv(zs2, shift0, st0, mu, w0, wdu_b, a0, wiu_b, wrg_b, k_k, k_a, r_k, gn_g, gn_b, ones_bd, tri, *,
          B, T, nbr, n_cr, C):
    tr = n_cr * C
    TT = nbr * tr
    assert T % tr == 0 and B % nbr == 0
    vec = lambda n: _const_spec((1, n))
    row3 = lambda b, t: (b, t, 0)
    ob, st1 = pl.pallas_call(
        functools.partial(_rwkv_kernel, nbr=nbr, n_cr=n_cr, C=C),
        grid=(B // nbr, T // tr),
        in_specs=[
            pl.BlockSpec((nbr, tr, SHIFT_COLS), row3),
            pl.BlockSpec((nbr, 1, SHIFT_COLS), lambda b, t: (b, 0, 0)),
            pl.BlockSpec((nbr, NG, GW, GW), lambda b, t: (b, 0, 0, 0)),
            vec(SHIFT_COLS), vec(D_RWKV), _const_spec((DECAY_RANK, D_RWKV)), vec(D_RWKV),
            _const_spec((ICLR_RANK, D_RWKV)), _const_spec((GATE_RANK, D_RWKV)),
            vec(D_RWKV), vec(D_RWKV), vec(D_RWKV), vec(D_RWKV), vec(D_RWKV),
            _const_spec((GW, GW)), _const_spec(tri.shape),
        ],
        out_specs=(
            pl.BlockSpec((nbr, tr, D_RWKV), row3),
            pl.BlockSpec((nbr, NG, GW, GW), lambda b, t: (b, 0, 0, 0)),
        ),
        out_shape=(
            jax.ShapeDtypeStruct((B, T, D_RWKV), BF16),
            jax.ShapeDtypeStruct((B, NG, GW, GW), F32),
        ),
        scratch_shapes=[pltpu.VMEM((nbr, 1, SHIFT_COLS), F32), pltpu.VMEM((nbr, NG, GW, GW), F32)],
        compiler_params=_cparams("parallel", "arbitrary"),
        name="rwkv7",
    )(zs2.reshape(B, T, SHIFT_COLS), shift0, st0, mu, w0, wdu_b, a0, wiu_b, wrg_b, k_k, k_a, r_k, gn_g, gn_b,
      ones_bd, tri)
    return ob.reshape(B * T, D_RWKV), st1


_FF_STEP = D_FF // 2


def _merge_ffn_kernel(x_ref, oa_ref, ob_ref, ga_ref, gb_ref, gt1_ref, sh_ref, sc_ref, gt2_ref, g_ref, gf_ref,
                      woa_ref, wob_ref, wo_ref, wi_ref, wfo_ref, o_ref, *, nb, tt, final):
    ma = ga_ref[...].astype(F32) * _dot(oa_ref[...], woa_ref[...])
    mb = gb_ref[...].astype(F32) * _dot(ob_ref[...], wob_ref[...])
    merged = (ma + mb).astype(BF16)
    x = x_ref[...] + _expand_mod(gt1_ref, nb, tt) * _dot(merged, wo_ref[...])
    h = _rms(x, g_ref[...]) * (1.0 + _expand_mod(sc_ref, nb, tt)) + _expand_mod(sh_ref, nb, tt)
    hb = h.astype(BF16)
    acc = None
    for a in range(0, D_FF, _FF_STEP):
        u = _dot(hb, wi_ref[:, a:a + _FF_STEP])
        gg = _dot(hb, wi_ref[:, D_FF + a:D_FF + a + _FF_STEP])
        act = (gg * _sigmoid(gg) * u).astype(BF16)
        d = _dot(act, wfo_ref[a:a + _FF_STEP, :])
        acc = d if acc is None else acc + d
    y = x + _expand_mod(gt2_ref, nb, tt) * acc
    if final:
        y = _rms(y, gf_ref[...])
    o_ref[...] = y


def _merge_ffn(x2, oa, ob, ga, gb, mod3, mrow, g, g_final, woa_b, wob_b, wo_b, wi_b, wfo_b, *, B, T, nb, tt, final):
    M = B * T
    tm = nb * tt
    nt = T // tt
    row = lambda b, t: (b * nt + t, 0)
    act = lambda n: pl.BlockSpec((tm, n), row)
    return pl.pallas_call(
        functools.partial(_merge_ffn_kernel, nb=nb, tt=tt, final=final),
        grid=(B // nb, nt),
        in_specs=[
            act(D_MODEL), act(D_ATTN), act(D_RWKV), act(D_MODEL), act(D_MODEL),
            _mod_spec(nb, mrow, _GT1), _mod_spec(nb, mrow, _SH2), _mod_spec(nb, mrow, _SC2),
            _mod_spec(nb, mrow, _GT2),
            _const_spec((1, D_MODEL)), _const_spec((1, D_MODEL)),
            _const_spec((D_ATTN, D_MODEL)), _const_spec((D_RWKV, D_MODEL)), _const_spec((D_MODEL, D_MODEL)),
            _const_spec((D_MODEL, 2 * D_FF)), _const_spec((D_FF, D_MODEL)),
        ],
        out_specs=act(D_MODEL),
        out_shape=jax.ShapeDtypeStruct((M, D_MODEL), F32),
        compiler_params=_cparams("parallel", "parallel"),
        name="merge_ffn",
    )(x2, oa, ob, ga, gb, mod3, mod3, mod3, mod3, g, g_final, woa_b, wob_b, wo_b, wi_b, wfo_b)


def _bias_kernel(rb_ref, o_ref, *, tq, band):
    n_rel = rb_ref.shape[1]
    parts = _split_bf16(rb_ref[...], 3)
    col = lax.broadcasted_iota(jnp.int32, (n_rel, band), 1)
    bucket = lax.broadcasted_iota(jnp.int32, (n_rel, band), 0)
    col1 = lax.broadcasted_iota(jnp.int32, (1, band), 1)
    for q in range(tq):
        idx = jnp.clip(col - (BAND_LEFT + q), -REL_CLIP, CHUNK - 1) + REL_CLIP
        onehot = jnp.where(idx == bucket, 1.0, 0.0).astype(BF16)
        acc = (_dot(parts[0], onehot) + _dot(parts[1], onehot)) + _dot(parts[2], onehot)
        lo = (q // CHUNK) * CHUNK
        o_ref[q] = jnp.where((col1 >= lo) & (col1 < lo + BAND), acc * LOG2E, NEG_INF)


def _bias_tables(rel_bias, tq):
    depth, n_heads, n_rel = rel_bias.shape
    r = depth * n_heads
    band = BAND_LEFT + tq
    t = pl.pallas_call(
        functools.partial(_bias_kernel, tq=tq, band=band),
        out_shape=jax.ShapeDtypeStruct((tq, r, band), F32),
        compiler_params=pltpu.CompilerParams(vmem_limit_bytes=VMEM_LIMIT_BYTES),
        name="bias_table",
    )(rel_bias.reshape(r, n_rel))
    return jnp.transpose(t, (1, 0, 2)).reshape(depth, n_heads, tq, band)


def _tile_consts(TT, C):
    cb = min(TT, CUM_BLOCK)
    assert TT % cb == 0 and cb % C == 0
    i = jnp.arange(cb)
    tri = ((i[:, None] >= i[None, :]) & (i[:, None] // C == i[None, :] // C)).astype(BF16)
    j = jnp.arange(GW)
    ones_bd = (j[:, None] // HEAD_DIM == j[None, :] // HEAD_DIM).astype(BF16)
    return ones_bd, tri


def _layer(x2, mod3, mrow, kbuf, vbuf, shift0, st0, lw, g_final, *, B, T, nb, tt, nbb, tq, n_sub, masked, nbr, n_cr,
           C, final):
    q, k_tail, v_tail, kbuf, vbuf, zs, ga, gb = _proj_in(
        x2, mod3, mrow, lw['g1'], lw['w_in'], kbuf, vbuf, B=B, T=T, nb=nb, tt=tt)
    oa = _attention(q, kbuf, vbuf, lw['bias'], B=B, T=T, nbb=nbb, tq=tq, n_sub=n_sub, masked=masked)
    ob, st1 = _rwkv(zs, shift0, st0, lw['mu'], lw['w0'], lw['wdu'], lw['a0'], lw['wiu'], lw['wrg'], lw['k_k'],
                    lw['k_a'], lw['r_k'], lw['gn_g'], lw['gn_b'], lw['ones_bd'], lw['tri'], B=B, T=T, nbr=nbr, n_cr=n_cr, C=C)
    x2 = _merge_ffn(x2, oa, ob, ga, gb, mod3, mrow, lw['g2'], g_final, lw['woa'], lw['wob'], lw['wo'], lw['wfi'],
                    lw['wfo'], B=B, T=T, nb=nb, tt=tt, final=final)
    return x2, k_tail, v_tail, kbuf, vbuf, st1, zs


def _path_config(B, T):
    n_keep = min(BAND_LEFT, T)
    tt = n_keep
    nb = 1 if T > tt else min(B, max(1, 256 // tt))
    if T >= CHUNK:
        tq = CHUNK * min(ATTN_GROUP_CHUNKS, T // CHUNK)
        nbb, n_sub, masked = 1, max(1, min(ATTN_STEP_ROWS, T) // tq), True
    else:
        nbb, tq, n_sub, masked = min(B, 8), T, 1, False
    C = min(CHUNK, T)
    n_cr = min(T // C, RWKV_ROW_CHUNKS)
    nbr = min(B, RWKV_MAX_BATCH_ROWS, max(1, RWKV_TILE_ROWS // (n_cr * C)))
    return dict(nb=nb, tt=tt, nbb=nbb, tq=tq, n_sub=n_sub, masked=masked, nbr=nbr, n_cr=n_cr, C=C)


def _layer_weights(l, p, cfgs, bias_all):
    row = lambda a: a.reshape(1, -1)
    lw = dict(
        g1=row(p['g_norm'][l, 0]), g2=row(p['g_norm'][l, 1]),
        w_in=p['w_in'][l].astype(BF16),
        mu=row(p['mu'][l]), w0=row(p['w0'][l]), wdu=p['w_decay_up'][l].astype(BF16), a0=row(p['a0'][l]),
        wiu=p['w_iclr_up'][l].astype(BF16), wrg=p['w_rg_up'][l].astype(BF16),
        k_k=row(p['k_k'][l]), k_a=row(p['k_a'][l]), r_k=row(p['r_k'][l]),
        gn_g=row(p['gn_g'][l]), gn_b=row(p['gn_b'][l]),
        woa=p['w_out_attn'][l].astype(BF16), wob=p['w_out_rwkv'][l].astype(BF16), wo=p['w_out'][l].astype(BF16),
        wfi=p['w_ffn_in'][l].astype(BF16), wfo=p['w_ffn_out'][l].astype(BF16),
    )
    out = []
    for cfg in cfgs:
        d = dict(lw)
        d['bias'] = bias_all[l, :, :cfg['tq'], :BAND_LEFT + cfg['tq']]
        d['ones_bd'], d['tri'] = _tile_consts(cfg['nbr'] * cfg['n_cr'] * cfg['C'], cfg['C'])
        out.append(d)
    return out


def _state_to_groups(s):
    b = s.shape[0]
    x = jnp.swapaxes(s, -1, -2).reshape(b, NG, GH, HEAD_DIM, HEAD_DIM)
    eye = jnp.eye(GH, dtype=s.dtype)
    return jnp.einsum('bgjkv,jJ->bgjkJv', x, eye).reshape(b, NG, GW, GW)


def _state_from_groups(st):
    b = st.shape[0]
    x = st.reshape(b, NG, GH, HEAD_DIM, GH, HEAD_DIM)
    d = jnp.stack([x[:, :, j, :, j, :] for j in range(GH)], axis=2)
    return jnp.swapaxes(d.reshape(b, N_HEADS, HEAD_DIM, HEAD_DIM), -1, -2)


def _zero_context_kernel(k_ref, v_ref):
    k_ref[...] = jnp.zeros(k_ref.shape, k_ref.dtype)
    v_ref[...] = jnp.zeros(v_ref.shape, v_ref.dtype)


def _empty_context(B, rows):
    spec = pl.BlockSpec((1, BAND_LEFT, D_ATTN), lambda b: (b, 0, 0))
    shape = jax.ShapeDtypeStruct((B, rows, D_ATTN), BF16)
    return pl.pallas_call(
        _zero_context_kernel, grid=(B,), out_specs=(spec, spec), out_shape=(shape, shape),
        compiler_params=_cparams("parallel"), name="zero_context",
    )()


def kernel(x_prompt, x_sample, cache_attn_k, cache_attn_v, state_wkv, state_shift, c_prompt, c_sample, w_ada, b_ada, g_norm, w_in, rel_bias, mu, w0, w_decay_up, a0, w_iclr_up, w_rg_up, k_k, k_a, r_k, gn_g, gn_b, w_out_attn, w_out_rwkv, w_out, w_ffn_in, w_ffn_out, g_final):
    p = dict(g_norm=g_norm, w_in=w_in, rel_bias=rel_bias, mu=mu, w0=w0, w_decay_up=w_decay_up, a0=a0,
             w_iclr_up=w_iclr_up, w_rg_up=w_rg_up, k_k=k_k, k_a=k_a, r_k=r_k, gn_g=gn_g, gn_b=gn_b,
             w_out_attn=w_out_attn, w_out_rwkv=w_out_rwkv, w_out=w_out, w_ffn_in=w_ffn_in, w_ffn_out=w_ffn_out)
    Bp, T, _ = x_prompt.shape
    Bs, Ts, _ = x_sample.shape
    W = cache_attn_k.shape[2]
    assert W == BAND_LEFT
    cfg_p, cfg_s = _path_config(Bp, T), _path_config(Bs, Ts)
    gf = g_final.reshape(1, D_MODEL)

    n_rows = Bp + Bs
    assert Bp % cfg_p['nb'] == 0 and n_rows % cfg_p['nb'] == 0 and Bp % cfg_s['nb'] == 0 and n_rows % cfg_s['nb'] == 0
    mod = _modulation(jnp.concatenate([c_prompt, c_sample], axis=0), w_ada, b_ada)
    mod3 = mod.reshape(DEPTH * n_rows, 1, 6 * D_MODEL)
    bias_all = _bias_tables(rel_bias, max(cfg_p['tq'], cfg_s['tq']))

    xp = x_prompt.reshape(Bp * T, D_MODEL)
    xs = x_sample.reshape(Bs * Ts, D_MODEL)
    kbuf_p, vbuf_p = _empty_context(Bp, BAND_LEFT + T)
    zero_shift = jnp.zeros((Bp, 1, SHIFT_COLS), F32)
    zero_state = jnp.zeros((Bp, NG, GW, GW), F32)
    n_keep = min(BAND_LEFT, T)

    outs = [[] for _ in range(8)]
    for l in range(DEPTH):
        lw_p, lw_s = _layer_weights(l, p, (cfg_p, cfg_s), bias_all)
        final = l == DEPTH - 1
        xp, k_t, v_t, kbuf_p, vbuf_p, st, zs = _layer(
            xp, mod3, l * n_rows // cfg_p['nb'], kbuf_p, vbuf_p, zero_shift, zero_state, lw_p, gf,
            B=Bp, T=T, final=final, **cfg_p)
        outs[0].append(k_t.reshape(Bp, n_keep, N_HEADS, HEAD_DIM))
        outs[1].append(v_t.reshape(Bp, n_keep, N_HEADS, HEAD_DIM))
        outs[2].append(_state_from_groups(st))
        outs[3].append(zs.reshape(Bp, T, SHIFT_COLS)[:, -1])

        pad = ((0, 0), (0, Ts), (0, 0))
        kbuf_s = jnp.pad(cache_attn_k[l].reshape(Bs, W, D_ATTN).astype(BF16), pad)
        vbuf_s = jnp.pad(cache_attn_v[l].reshape(Bs, W, D_ATTN).astype(BF16), pad)
        xs, k_t, v_t, _, _, st, zs = _layer(
            xs, mod3, (l * n_rows + Bp) // cfg_s['nb'], kbuf_s, vbuf_s, state_shift[l].reshape(Bs, 1, SHIFT_COLS),
            _state_to_groups(state_wkv[l]), lw_s, gf, B=Bs, T=Ts, final=final, **cfg_s)
        outs[4].append(k_t.reshape(Bs, Ts, N_HEADS, HEAD_DIM))
        outs[5].append(v_t.reshape(Bs, Ts, N_HEADS, HEAD_DIM))
        outs[6].append(_state_from_groups(st))
        outs[7].append(zs.reshape(Bs, Ts, SHIFT_COLS)[:, -1])

    y_prompt = xp.reshape(Bp, T, D_MODEL)
    y_sample = xs.reshape(Bs, Ts, D_MODEL)
    return (y_prompt, y_sample) + tuple(jnp.stack(o) for o in outs)
```

```python
import functools

import jax
import jax.numpy as jnp
from jax import lax
from jax.experimental import pallas as pl
from jax.experimental.pallas import tpu as pltpu

D_MODEL = 1024
DEPTH = 4
CHUNK = 64
BAND_LEFT = 512
BAND = BAND_LEFT + CHUNK
N_HEADS = 8
HEAD_DIM = 64
D_ATTN = 512
D_RWKV = 512
REL_CLIP = 128
DECAY_RANK = 64
ICLR_RANK = 64
GATE_RANK = 128
SHIFT_COLS = 3 * D_RWKV + DECAY_RANK + ICLR_RANK + GATE_RANK
PROJ_COLS = 3 * D_ATTN + SHIFT_COLS + 2 * D_MODEL
D_FF = 2816
RMS_EPS = 1e-6
GN_EPS = 64e-5
NEG_INF = -1e30
LOG2E = 1.4426950408889634

F32 = jnp.float32
BF16 = jnp.bfloat16

RWKV_TILE_ROWS = 512
RWKV_ROW_CHUNKS = 4
RWKV_MAX_BATCH_ROWS = 8
ATTN_GROUP_CHUNKS = 2
ATTN_STEP_ROWS = 256
SUM_TERMS = 1
CUM_TERMS = 2
CUM_BLOCK = 256

GH = 4
GW = GH * HEAD_DIM
NG = N_HEADS // GH

VMEM_LIMIT_BYTES = 56 * 1024 * 1024


def _cparams(*sem):
    return pltpu.CompilerParams(dimension_semantics=sem, vmem_limit_bytes=VMEM_LIMIT_BYTES)


def _const_spec(shape):
    nd = len(shape)
    return pl.BlockSpec(shape, lambda *_: (0,) * nd, pipeline_mode=pl.Buffered(1))


def _dot(a, b):
    return jnp.dot(a, b, preferred_element_type=F32)


def _dot_nt(a, b):
    return lax.dot_general(a, b, (((1,), (1,)), ((), ())), preferred_element_type=F32)


def _dot_tn(a, b):
    return lax.dot_general(a, b, (((0,), (0,)), ((), ())), preferred_element_type=F32)


def _sigmoid(x):
    return 1.0 / (1.0 + jnp.exp(-x))


def _expand_mod(ref, nb, tt):
    m = ref[...]
    if nb == 1:
        return m[0]
    return jnp.broadcast_to(m, (nb, tt, m.shape[-1])).reshape(nb * tt, m.shape[-1])


def _rms(x, g):
    ms = jnp.mean(x * x, axis=-1, keepdims=True)
    return x * lax.rsqrt(ms + RMS_EPS) * g


def _mod_kernel(c_ref, w_ref, b_ref, o_ref):
    c = c_ref[...]
    s = (c * _sigmoid(c)).astype(BF16)
    o_ref[0] = _dot(s, w_ref[0].astype(BF16)) + b_ref[0]


def _modulation(c_all, w_ada, b_ada):
    nb = c_all.shape[0]
    tn = 1536
    return pl.pallas_call(
        _mod_kernel,
        grid=(DEPTH, 6 * D_MODEL // tn),
        in_specs=[
            pl.BlockSpec((nb, D_MODEL), lambda l, j: (0, 0)),
            pl.BlockSpec((1, D_MODEL, tn), lambda l, j: (l, 0, j)),
            pl.BlockSpec((1, 1, tn), lambda l, j: (l, 0, j)),
        ],
        out_specs=pl.BlockSpec((1, nb, tn), lambda l, j: (l, 0, j)),
        out_shape=jax.ShapeDtypeStruct((DEPTH, nb, 6 * D_MODEL), F32),
        compiler_params=_cparams("parallel", "parallel"),
        name="adaln_mod",
    )(c_all, w_ada, b_ada.reshape(DEPTH, 1, 6 * D_MODEL))


_Q0, _K0, _V0, _ZS0, _GA0, _GB0 = 0, 512, 1024, 1536, 3328, 4352
_COL_STEP = 512


def _proj_in_kernel(x_ref, sh_ref, sc_ref, g_ref, w_ref, kin_ref, vin_ref,
                    q_ref, k_ref, v_ref, kb_ref, vb_ref, zs_ref, ga_ref, gb_ref, *, nb, tt):
    del kin_ref, vin_ref
    x = x_ref[...]
    h = _rms(x, g_ref[...]) * (1.0 + _expand_mod(sc_ref, nb, tt)) + _expand_mod(sh_ref, nb, tt)
    hb = h.astype(BF16)

    def cols(a, b):
        return _dot(hb, w_ref[:, a:b])

    q_ref[...] = (cols(_Q0, _K0) * (HEAD_DIM ** -0.5 * LOG2E)).astype(BF16)
    k = cols(_K0, _V0)
    k_ref[...] = k.reshape(k_ref.shape)
    kb_ref[...] = k.astype(BF16).reshape(kb_ref.shape)
    v = cols(_V0, _ZS0)
    v_ref[...] = v.reshape(v_ref.shape)
    vb_ref[...] = v.astype(BF16).reshape(vb_ref.shape)
    for a in range(_ZS0, _GA0, _COL_STEP):
        b = min(a + _COL_STEP, _GA0)
        zs_ref[:, a - _ZS0:b - _ZS0] = cols(a, b)
    for a in range(_GA0, _GB0, _COL_STEP):
        ga_ref[:, a - _GA0:a - _GA0 + _COL_STEP] = _sigmoid(cols(a, a + _COL_STEP)).astype(BF16)
    for a in range(_GB0, PROJ_COLS, _COL_STEP):
        gb_ref[:, a - _GB0:a - _GB0 + _COL_STEP] = _sigmoid(cols(a, a + _COL_STEP)).astype(BF16)


_SH1, _SC1, _GT1, _SH2, _SC2, _GT2 = range(6)


def _mod_spec(nb, mrow, which):
    return pl.BlockSpec((nb, 1, D_MODEL), lambda b, t: (mrow + b, 0, which))


def _proj_in(x2, mod3, mrow, g, w_in_b, kbuf, vbuf, *, B, T, nb, tt):
    M = B * T
    tm = nb * tt
    n_keep = min(BAND_LEFT, T)
    assert tt == n_keep and T % tt == 0 and B % nb == 0 and BAND_LEFT % tt == 0
    assert nb == 1 or tt == T
    nt = T // tt
    pad_blocks = BAND_LEFT // tt
    row = lambda b, t: (b * nt + t, 0)
    out_shapes = (
        jax.ShapeDtypeStruct((M, D_ATTN), BF16),
        jax.ShapeDtypeStruct((B, n_keep, D_ATTN), F32),
        jax.ShapeDtypeStruct((B, n_keep, D_ATTN), F32),
        jax.ShapeDtypeStruct(kbuf.shape, BF16),
        jax.ShapeDtypeStruct(vbuf.shape, BF16),
        jax.ShapeDtypeStruct((M, SHIFT_COLS), F32),
        jax.ShapeDtypeStruct((M, D_MODEL), BF16),
        jax.ShapeDtypeStruct((M, D_MODEL), BF16),
    )
    tail = pl.BlockSpec((nb, tt, D_ATTN), lambda b, t: (b, 0, 0))
    behind = pl.BlockSpec((nb, tt, D_ATTN), lambda b, t: (b, t + pad_blocks, 0))
    return pl.pallas_call(
        functools.partial(_proj_in_kernel, nb=nb, tt=tt),
        grid=(B // nb, nt),
        in_specs=[
            pl.BlockSpec((tm, D_MODEL), row), _mod_spec(nb, mrow, _SH1), _mod_spec(nb, mrow, _SC1),
            _const_spec((1, D_MODEL)), _const_spec((D_MODEL, PROJ_COLS)),
            pl.BlockSpec(memory_space=pl.ANY), pl.BlockSpec(memory_space=pl.ANY),
        ],
        out_specs=(
            pl.BlockSpec((tm, D_ATTN), row), tail, tail, behind, behind,
            pl.BlockSpec((tm, SHIFT_COLS), row), pl.BlockSpec((tm, D_MODEL), row), pl.BlockSpec((tm, D_MODEL), row),
        ),
        out_shape=out_shapes,
        input_output_aliases={5: 3, 6: 4},
        compiler_params=_cparams("parallel", "arbitrary"),
        name="proj_in",
    )(x2, mod3, mod3, g, w_in_b, kbuf, vbuf)


def _attn_kernel(q_ref, k_ref, v_ref, bias_ref, o_ref, *, nbb, tq, n_sub, band, masked):
    j = pl.program_id(1)
    sls = [slice(h * HEAD_DIM, (h + 1) * HEAD_DIM) for h in range(N_HEADS)]
    qs, kbs, vbs, valids = [], [], [], []
    for bb in range(nbb):
        for i in range(n_sub):
            start = pl.multiple_of((j * n_sub + i) * tq, tq)
            r0 = (bb * n_sub + i) * tq
            qs.append(q_ref[r0:r0 + tq, :])
            kbs.append(k_ref[bb, pl.ds(start, band), :])
            vbs.append(v_ref[bb, pl.ds(start, band), :])
            valids.append(lax.broadcasted_iota(jnp.int32, (1, band), 1) >= BAND_LEFT - start)
    n = nbb * n_sub
    ss = [[_dot_nt(qs[i][:, sl], kbs[i][:, sl]) + bias_ref[h] for h, sl in enumerate(sls)] for i in range(n)]
    if masked:
        ss = [[jnp.where(valids[i], s, NEG_INF) for s in ss[i]] for i in range(n)]
    ps = [[jnp.exp2(s - jnp.max(s, axis=-1, keepdims=True)) for s in ss[i]] for i in range(n)]
    ls = [[jnp.sum(p, axis=-1, keepdims=True) for p in ps[i]] for i in range(n)]
    os_ = [[_dot(ps[i][h].astype(BF16), vbs[i][:, sl]) for h, sl in enumerate(sls)] for i in range(n)]
    for i in range(n):
        o = jnp.concatenate([os_[i][h] / ls[i][h] for h in range(N_HEADS)], axis=-1)
        o_ref[i * tq:(i + 1) * tq, :] = o.astype(BF16)


def _attention(q2, kbuf, vbuf, bias, *, B, T, nbb, tq, n_sub, masked):
    band = bias.shape[-1]
    Tp = kbuf.shape[1]
    nj = T // (tq * n_sub)
    assert nbb == 1 or nj == 1
    rows = nbb * tq * n_sub
    row = lambda b, j: (b * nj + j, 0)
    kv = pl.BlockSpec((nbb, Tp, D_ATTN), lambda b, j: (b, 0, 0))
    return pl.pallas_call(
        functools.partial(_attn_kernel, nbb=nbb, tq=tq, n_sub=n_sub, band=band, masked=masked),
        grid=(B // nbb, nj),
        in_specs=[pl.BlockSpec((rows, D_ATTN), row), kv, kv, _const_spec(bias.shape)],
        out_specs=pl.BlockSpec((rows, D_ATTN), row),
        out_shape=jax.ShapeDtypeStruct((B * T, D_ATTN), BF16),
        compiler_params=_cparams("parallel", "arbitrary"),
        name="band_attn",
    )(q2, kbuf, vbuf, bias)


def _split_bf16(x, n):
    parts = []
    for _ in range(n):
        p = x.astype(BF16)
        parts.append(p)
        x = x - p.astype(F32)
    return parts


def _rwkv_kernel(zs_ref, sh0_ref, s0_ref, mu_ref, w0_ref, wdu_ref, a0_ref, wiu_ref, wrg_ref, kk_ref, ka_ref,
                 rk_ref, gng_ref, gnb_ref, ones_ref, tri_ref,
                 ob_ref, sout_ref, carry_ref, st_ref, *, nbr, n_cr, C):
    tr = n_cr * C
    TT = nbr * tr

    @pl.when(pl.program_id(1) == 0)
    def _():
        carry_ref[...] = sh0_ref[...]
        st_ref[...] = s0_ref[...]

    zs3 = zs_ref[...]
    zs = zs3.reshape(TT, SHIFT_COLS)
    rows = lax.broadcasted_iota(jnp.int32, (TT, 1), 0)
    prev = jnp.where(rows % tr == 0, _expand_mod(carry_ref, nbr, tr), pltpu.roll(zs, 1, 0))
    carry_ref[...] = zs3[:, tr - 1:tr, :]
    zx = zs + (prev - zs) * mu_ref[...]

    r = zx[:, 0:D_RWKV]
    kx = zx[:, D_RWKV:2 * D_RWKV]
    v = zx[:, 2 * D_RWKV:3 * D_RWKV]
    o = 3 * D_RWKV
    wd = zx[:, o:o + DECAY_RANK]
    ad = zx[:, o + DECAY_RANK:o + DECAY_RANK + ICLR_RANK]
    gd = zx[:, o + DECAY_RANK + ICLR_RANK:]

    ones_bd = ones_ref[...]

    def segsum(x, n_terms):
        halves = []
        for gi in range(NG):
            acc = None
            for part in _split_bf16(x[:, gi * GW:(gi + 1) * GW], n_terms):
                d = _dot(part, ones_bd)
                acc = d if acc is None else acc + d
            halves.append(acc)
        return jnp.concatenate(halves, axis=-1)

    xw = w0_ref[...] + _dot(jnp.tanh(wd).astype(BF16), wdu_ref[...])
    w_log = -(jnp.maximum(-xw, 0.0) + jnp.log(1.0 + jnp.exp(-jnp.abs(xw)))) - 0.5
    lw = -jnp.exp(w_log)
    a = _sigmoid(a0_ref[...] + _dot(ad.astype(BF16), wiu_ref[...]))
    g = _dot(_sigmoid(gd).astype(BF16), wrg_ref[...])
    kkv = kx * kk_ref[...]
    kap = kkv * lax.rsqrt(jnp.maximum(segsum(kkv * kkv, SUM_TERMS), 1e-24))
    kh = kx * (1.0 + (a - 1.0) * ka_ref[...])
    bb = kap * a

    tri = tri_ref[...]
    cb = tri.shape[0]
    cums = []
    for r0 in range(0, TT, cb):
        acc = None
        for part in _split_bf16(lw[r0:r0 + cb], CUM_TERMS):
            d = _dot(tri, part)
            acc = d if acc is None else acc + d
        cums.append(acc)
    cum = cums[0] if len(cums) == 1 else jnp.concatenate(cums, axis=0)

    ecp = jnp.exp(cum - lw)
    kt_all = kap * ecp
    rt_all = r * (ecp * jnp.exp(lw))
    en = jnp.exp(-cum)
    kh_all = kh * en
    bh_all = bb * en

    n_ch = nbr * n_cr
    cw = GH * C
    n_sq = C.bit_length() - 2
    t_cat = lax.broadcasted_iota(jnp.int32, (C, cw), 0)
    l_cat = lax.broadcasted_iota(jnp.int32, (C, cw), 1)
    s_cat = l_cat - (l_cat // C) * C
    strict = t_cat > s_cat
    incl = t_cat >= s_cat
    eye_cat = (t_cat == s_cat).astype(F32)

    def head_mask(n_rows, rows_per_head, n_cols, cols_per_head):
        ri = lax.broadcasted_iota(jnp.int32, (n_rows, n_cols), 0) // rows_per_head
        ci = lax.broadcasted_iota(jnp.int32, (n_rows, n_cols), 1) // cols_per_head
        return ri == ci

    m_tk = head_mask(cw, C, GW, HEAD_DIM)
    m_tt = m_tk if C == HEAD_DIM else head_mask(cw, C, cw, C)
    m_kk = m_tk if C == HEAD_DIM else head_mask(GW, HEAD_DIM, GW, HEAD_DIM)
    diag_kk = (lax.broadcasted_iota(jnp.int32, (GW, GW), 0) == lax.broadcasted_iota(jnp.int32, (GW, GW), 1))

    def bdiag(x, mask):
        return jnp.where(mask, jnp.concatenate([x] * GH, axis=0), jnp.zeros((), x.dtype))

    pcs = [jnp.exp(cum[(c + 1) * C - 1:(c + 1) * C, :]) for c in range(n_ch)]
    to_end = lambda x: jnp.concatenate([x[c * C:(c + 1) * C] * pcs[c] for c in range(n_ch)], axis=0).astype(BF16)
    kd_b = to_end(kh_all)
    bd_b = to_end(bh_all)
    v_b = v.astype(BF16)
    kt_b = kt_all.astype(BF16)
    rt_b = rt_all.astype(BF16)
    khb = kh_all.astype(BF16)
    bhb = bh_all.astype(BF16)

    probs = [(c, gi) for c in range(n_ch) for gi in range(NG)]
    rsl = lambda c: slice(c * C, (c + 1) * C)
    gsl = lambda gi: slice(gi * GW, (gi + 1) * GW)
    cut = lambda arr, pr: arr[rsl(pr[0]), gsl(pr[1])]
    xs = [jnp.concatenate([cut(kt_b, pr), cut(rt_b, pr)], axis=0) for pr in probs]
    xks = [_dot_nt(x, bdiag(cut(khb, pr), m_tk)) for x, pr in zip(xs, probs)]
    xbs = [_dot_nt(x, bdiag(cut(bhb, pr), m_tk)) for x, pr in zip(xs, probs)]
    vbds = [bdiag(cut(v_b, pr), m_tk) for pr in probs]
    mas = [jnp.concatenate([jnp.where(strict, xk[:C], 0.0), jnp.where(incl, xk[C:], 0.0)], axis=0).astype(BF16)
           for xk in xks]
    ps = [jnp.where(strict, xb[:C], 0.0) for xb in xbs]
    arbs = [jnp.where(incl, xb[C:], 0.0).astype(BF16) for xb in xbs]
    mavs = [_dot(ma, vbd) for ma, vbd in zip(mas, vbds)]
    tms = [eye_cat - p for p in ps]
    pbds = [bdiag(p.astype(BF16), m_tt) for p in ps]
    for _ in range(n_sq):
        ps = [_dot(p.astype(BF16), pbd) for p, pbd in zip(ps, pbds)]
        pbds = [bdiag(p.astype(BF16), m_tt) for p in ps]
        tms = [tm + _dot(tm.astype(BF16), pbd) for tm, pbd in zip(tms, pbds)]
    tmbs = [tm.astype(BF16) for tm in tms]
    tks = [_dot(tmb, bdiag(cut(kt_b, pr), m_tk)).astype(BF16) for tmb, pr in zip(tmbs, probs)]
    tmvs = [_dot(tmb, bdiag(mav[:C].astype(BF16), m_tk)).astype(BF16) for tmb, mav in zip(tmbs, mavs)]
    qps = [(cut(rt_all, pr) - _dot(arb, bdiag(tk, m_tk))).astype(BF16) for pr, arb, tk in zip(probs, arbs, tks)]
    y0s = [mav[C:] - _dot(arb, bdiag(tmv, m_tk)) for mav, arb, tmv in zip(mavs, arbs, tmvs)]
    gms, hms = [], []
    for i, pr in enumerate(probs):
        bdc = cut(bd_b, pr)
        gfull = jnp.where(diag_kk, pcs[pr[0]][:, gsl(pr[1])], 0.0) - _dot_tn(bdc, tks[i])
        gms.append(jnp.where(m_kk, gfull, 0.0).astype(BF16))
        hfull = _dot_tn(jnp.concatenate([cut(kd_b, pr), bdc], axis=0),
                        jnp.concatenate([cut(v_b, pr), -tmvs[i]], axis=0))
        hms.append(jnp.where(m_kk, hfull, 0.0))
    sts = [[st_ref[b, gi] for gi in range(NG)] for b in range(nbr)]
    ys = [[None] * NG for _ in range(n_ch)]
    for cc in range(n_cr):
        for b in range(nbr):
            for gi in range(NG):
                c = b * n_cr + cc
                i = c * NG + gi
                r2 = _dot(jnp.concatenate([qps[i], gms[i]], axis=0), sts[b][gi].astype(BF16))
                ys[c][gi] = r2[:C] + y0s[i]
                sts[b][gi] = r2[C:] + hms[i]
    for b in range(nbr):
        for gi in range(NG):
            st_ref[b, gi] = sts[b][gi]
            sout_ref[b, gi] = sts[b][gi]
    y = jnp.concatenate([jnp.concatenate(yc, axis=-1) for yc in ys], axis=0)

    inv_n = 1.0 / HEAD_DIM
    mean = segsum(y, SUM_TERMS) * inv_n
    dlt = y - mean
    var = segsum(dlt * dlt, SUM_TERMS) * inv_n
    yn = dlt * lax.rsqrt(var + GN_EPS) * gng_ref[...] + gnb_ref[...]
    yn = yn + segsum(r * kh * rk_ref[...], SUM_TERMS) * v
    ob_ref[...] = (yn * g).astype(BF16).reshape(ob_ref.shape)


def _rwkv(zs2, shift0, st0, mu, w0, wdu_b, a0, wiu_b, wrg_b, k_k, k_a, r_k, gn_g, gn_b, ones_bd, tri, *,
          B, T, nbr, n_cr, C):
    tr = n_cr * C
    TT = nbr * tr
    assert T % tr == 0 and B % nbr == 0
    vec = lambda n: _const_spec((1, n))
    row3 = lambda b, t: (b, t, 0)
    ob, st1 = pl.pallas_call(
        functools.partial(_rwkv_kernel, nbr=nbr, n_cr=n_cr, C=C),
        grid=(B // nbr, T // tr),
        in_specs=[
            pl.BlockSpec((nbr, tr, SHIFT_COLS), row3),
            pl.BlockSpec((nbr, 1, SHIFT_COLS), lambda b, t: (b, 0, 0)),
            pl.BlockSpec((nbr, NG, GW, GW), lambda b, t: (b, 0, 0, 0)),
            vec(SHIFT_COLS), vec(D_RWKV), _const_spec((DECAY_RANK, D_RWKV)), vec(D_RWKV),
            _const_spec((ICLR_RANK, D_RWKV)), _const_spec((GATE_RANK, D_RWKV)),
            vec(D_RWKV), vec(D_RWKV), vec(D_RWKV), vec(D_RWKV), vec(D_RWKV),
            _const_spec((GW, GW)), _const_spec(tri.shape),
        ],
        out_specs=(
            pl.BlockSpec((nbr, tr, D_RWKV), row3),
            pl.BlockSpec((nbr, NG, GW, GW), lambda b, t: (b, 0, 0, 0)),
        ),
        out_shape=(
            jax.ShapeDtypeStruct((B, T, D_RWKV), BF16),
            jax.ShapeDtypeStruct((B, NG, GW, GW), F32),
        ),
        scratch_shapes=[pltpu.VMEM((nbr, 1, SHIFT_COLS), F32), pltpu.VMEM((nbr, NG, GW, GW), F32)],
        compiler_params=_cparams("parallel", "arbitrary"),
        name="rwkv7",
    )(zs2.reshape(B, T, SHIFT_COLS), shift0, st0, mu, w0, wdu_b, a0, wiu_b, wrg_b, k_k, k_a, r_k, gn_g, gn_b,
      ones_bd, tri)
    return ob.reshape(B * T, D_RWKV), st1


_FF_STEP = D_FF // 2


def _merge_ffn_kernel(x_ref, oa_ref, ob_ref, ga_ref, gb_ref, gt1_ref, sh_ref, sc_ref, gt2_ref, g_ref, gf_ref,
                      woa_ref, wob_ref, wo_ref, wi_ref, wfo_ref, o_ref, *, nb, tt, final):
    ma = ga_ref[...].astype(F32) * _dot(oa_ref[...], woa_ref[...])
    mb = gb_ref[...].astype(F32) * _dot(ob_ref[...], wob_ref[...])
    merged = (ma + mb).astype(BF16)
    x = x_ref[...] + _expand_mod(gt1_ref, nb, tt) * _dot(merged, wo_ref[...])
    h = _rms(x, g_ref[...]) * (1.0 + _expand_mod(sc_ref, nb, tt)) + _expand_mod(sh_ref, nb, tt)
    hb = h.astype(BF16)
    acc = None
    for a in range(0, D_FF, _FF_STEP):
        u = _dot(hb, wi_ref[:, a:a + _FF_STEP])
        gg = _dot(hb, wi_ref[:, D_FF + a:D_FF + a + _FF_STEP])
        act = (gg * _sigmoid(gg) * u).astype(BF16)
        d = _dot(act, wfo_ref[a:a + _FF_STEP, :])
        acc = d if acc is None else acc + d
    y = x + _expand_mod(gt2_ref, nb, tt) * acc
    if final:
        y = _rms(y, gf_ref[...])
    o_ref[...] = y


def _merge_ffn(x2, oa, ob, ga, gb, mod3, mrow, g, g_final, woa_b, wob_b, wo_b, wi_b, wfo_b, *, B, T, nb, tt, final):
    M = B * T
    tm = nb * tt
    nt = T // tt
    row = lambda b, t: (b * nt + t, 0)
    act = lambda n: pl.BlockSpec((tm, n), row)
    return pl.pallas_call(
        functools.partial(_merge_ffn_kernel, nb=nb, tt=tt, final=final),
        grid=(B // nb, nt),
        in_specs=[
            act(D_MODEL), act(D_ATTN), act(D_RWKV), act(D_MODEL), act(D_MODEL),
            _mod_spec(nb, mrow, _GT1), _mod_spec(nb, mrow, _SH2), _mod_spec(nb, mrow, _SC2),
            _mod_spec(nb, mrow, _GT2),
            _const_spec((1, D_MODEL)), _const_spec((1, D_MODEL)),
            _const_spec((D_ATTN, D_MODEL)), _const_spec((D_RWKV, D_MODEL)), _const_spec((D_MODEL, D_MODEL)),
            _const_spec((D_MODEL, 2 * D_FF)), _const_spec((D_FF, D_MODEL)),
        ],
        out_specs=act(D_MODEL),
        out_shape=jax.ShapeDtypeStruct((M, D_MODEL), F32),
        compiler_params=_cparams("parallel", "parallel"),
        name="merge_ffn",
    )(x2, oa, ob, ga, gb, mod3, mod3, mod3, mod3, g, g_final, woa_b, wob_b, wo_b, wi_b, wfo_b)


def _bias_kernel(rb_ref, o_ref, *, tq, band):
    n_rel = rb_ref.shape[1]
    parts = _split_bf16(rb_ref[...], 3)
    col = lax.broadcasted_iota(jnp.int32, (n_rel, band), 1)
    bucket = lax.broadcasted_iota(jnp.int32, (n_rel, band), 0)
    col1 = lax.broadcasted_iota(jnp.int32, (1, band), 1)
    for q in range(tq):
        idx = jnp.clip(col - (BAND_LEFT + q), -REL_CLIP, CHUNK - 1) + REL_CLIP
        onehot = jnp.where(idx == bucket, 1.0, 0.0).astype(BF16)
        acc = (_dot(parts[0], onehot) + _dot(parts[1], onehot)) + _dot(parts[2], onehot)
        lo = (q // CHUNK) * CHUNK
        o_ref[q] = jnp.where((col1 >= lo) & (col1 < lo + BAND), acc * LOG2E, NEG_INF)


def _bias_tables(rel_bias, tq):
    depth, n_heads, n_rel = rel_bias.shape
    r = depth * n_heads
    band = BAND_LEFT + tq
    t = pl.pallas_call(
        functools.partial(_bias_kernel, tq=tq, band=band),
        out_shape=jax.ShapeDtypeStruct((tq, r, band), F32),
        compiler_params=pltpu.CompilerParams(vmem_limit_bytes=VMEM_LIMIT_BYTES),
        name="bias_table",
    )(rel_bias.reshape(r, n_rel))
    return jnp.transpose(t, (1, 0, 2)).reshape(depth, n_heads, tq, band)


def _tile_consts(TT, C):
    cb = min(TT, CUM_BLOCK)
    assert TT % cb == 0 and cb % C == 0
    i = jnp.arange(cb)
    tri = ((i[:, None] >= i[None, :]) & (i[:, None] // C == i[None, :] // C)).astype(BF16)
    j = jnp.arange(GW)
    ones_bd = (j[:, None] // HEAD_DIM == j[None, :] // HEAD_DIM).astype(BF16)
    return ones_bd, tri


def _layer(x2, mod3, mrow, kbuf, vbuf, shift0, st0, lw, g_final, *, B, T, nb, tt, nbb, tq, n_sub, masked, nbr, n_cr,
           C, final):
    q, k_tail, v_tail, kbuf, vbuf, zs, ga, gb = _proj_in(
        x2, mod3, mrow, lw['g1'], lw['w_in'], kbuf, vbuf, B=B, T=T, nb=nb, tt=tt)
    oa = _attention(q, kbuf, vbuf, lw['bias'], B=B, T=T, nbb=nbb, tq=tq, n_sub=n_sub, masked=masked)
    ob, st1 = _rwkv(zs, shift0, st0, lw['mu'], lw['w0'], lw['wdu'], lw['a0'], lw['wiu'], lw['wrg'], lw['k_k'],
                    lw['k_a'], lw['r_k'], lw['gn_g'], lw['gn_b'], lw['ones_bd'], lw['tri'], B=B, T=T, nbr=nbr, n_cr=n_cr, C=C)
    x2 = _merge_ffn(x2, oa, ob, ga, gb, mod3, mrow, lw['g2'], g_final, lw['woa'], lw['wob'], lw['wo'], lw['wfi'],
                    lw['wfo'], B=B, T=T, nb=nb, tt=tt, final=final)
    return x2, k_tail, v_tail, kbuf, vbuf, st1, zs


def _path_config(B, T):
    n_keep = min(BAND_LEFT, T)
    tt = n_keep
    nb = 1 if T > tt else min(B, max(1, 256 // tt))
    if T >= CHUNK:
        tq = CHUNK * min(ATTN_GROUP_CHUNKS, T // CHUNK)
        nbb, n_sub, masked = 1, max(1, min(ATTN_STEP_ROWS, T) // tq), True
    else:
        nbb, tq, n_sub, masked = min(B, 8), T, 1, False
    C = min(CHUNK, T)
    n_cr = min(T // C, RWKV_ROW_CHUNKS)
    nbr = min(B, RWKV_MAX_BATCH_ROWS, max(1, RWKV_TILE_ROWS // (n_cr * C)))
    return dict(nb=nb, tt=tt, nbb=nbb, tq=tq, n_sub=n_sub, masked=masked, nbr=nbr, n_cr=n_cr, C=C)


def _layer_weights(l, p, cfgs, bias_all):
    row = lambda a: a.reshape(1, -1)
    lw = dict(
        g1=row(p['g_norm'][l, 0]), g2=row(p['g_norm'][l, 1]),
        w_in=p['w_in'][l].astype(BF16),
        mu=row(p['mu'][l]), w0=row(p['w0'][l]), wdu=p['w_decay_up'][l].astype(BF16), a0=row(p['a0'][l]),
        wiu=p['w_iclr_up'][l].astype(BF16), wrg=p['w_rg_up'][l].astype(BF16),
        k_k=row(p['k_k'][l]), k_a=row(p['k_a'][l]), r_k=row(p['r_k'][l]),
        gn_g=row(p['gn_g'][l]), gn_b=row(p['gn_b'][l]),
        woa=p['w_out_attn'][l].astype(BF16), wob=p['w_out_rwkv'][l].astype(BF16), wo=p['w_out'][l].astype(BF16),
        wfi=p['w_ffn_in'][l].astype(BF16), wfo=p['w_ffn_out'][l].astype(BF16),
    )
    out = []
    for cfg in cfgs:
        d = dict(lw)
        d['bias'] = bias_all[l, :, :cfg['tq'], :BAND_LEFT + cfg['tq']]
        d['ones_bd'], d['tri'] = _tile_consts(cfg['nbr'] * cfg['n_cr'] * cfg['C'], cfg['C'])
        out.append(d)
    return out


def _state_to_groups(s):
    b = s.shape[0]
    x = jnp.swapaxes(s, -1, -2).reshape(b, NG, GH, HEAD_DIM, HEAD_DIM)
    eye = jnp.eye(GH, dtype=s.dtype)
    return jnp.einsum('bgjkv,jJ->bgjkJv', x, eye).reshape(b, NG, GW, GW)


def _state_from_groups(st):
    b = st.shape[0]
    x = st.reshape(b, NG, GH, HEAD_DIM, GH, HEAD_DIM)
    d = jnp.stack([x[:, :, j, :, j, :] for j in range(GH)], axis=2)
    return jnp.swapaxes(d.reshape(b, N_HEADS, HEAD_DIM, HEAD_DIM), -1, -2)


def _zero_context_kernel(k_ref, v_ref):
    k_ref[...] = jnp.zeros(k_ref.shape, k_ref.dtype)
    v_ref[...] = jnp.zeros(v_ref.shape, v_ref.dtype)


def _empty_context(B, rows):
    assert rows % BAND_LEFT == 0
    spec = pl.BlockSpec((1, BAND_LEFT, D_ATTN), lambda b, t: (b, t, 0))
    shape = jax.ShapeDtypeStruct((B, rows, D_ATTN), BF16)
    return pl.pallas_call(
        _zero_context_kernel, grid=(B, rows // BAND_LEFT), out_specs=(spec, spec), out_shape=(shape, shape),
        compiler_params=_cparams("parallel", "parallel"), name="zero_context",
    )()


def kernel(x_prompt, x_sample, cache_attn_k, cache_attn_v, state_wkv, state_shift, c_prompt, c_sample, w_ada, b_ada, g_norm, w_in, rel_bias, mu, w0, w_decay_up, a0, w_iclr_up, w_rg_up, k_k, k_a, r_k, gn_g, gn_b, w_out_attn, w_out_rwkv, w_out, w_ffn_in, w_ffn_out, g_final):
    p = dict(g_norm=g_norm, w_in=w_in, rel_bias=rel_bias, mu=mu, w0=w0, w_decay_up=w_decay_up, a0=a0,
             w_iclr_up=w_iclr_up, w_rg_up=w_rg_up, k_k=k_k, k_a=k_a, r_k=r_k, gn_g=gn_g, gn_b=gn_b,
             w_out_attn=w_out_attn, w_out_rwkv=w_out_rwkv, w_out=w_out, w_ffn_in=w_ffn_in, w_ffn_out=w_ffn_out)
    Bp, T, _ = x_prompt.shape
    Bs, Ts, _ = x_sample.shape
    W = cache_attn_k.shape[2]
    assert W == BAND_LEFT
    cfg_p, cfg_s = _path_config(Bp, T), _path_config(Bs, Ts)
    gf = g_final.reshape(1, D_MODEL)

    n_rows = Bp + Bs
    assert Bp % cfg_p['nb'] == 0 and n_rows % cfg_p['nb'] == 0 and Bp % cfg_s['nb'] == 0 and n_rows % cfg_s['nb'] == 0
    mod = _modulation(jnp.concatenate([c_prompt, c_sample], axis=0), w_ada, b_ada)
    mod3 = mod.reshape(DEPTH * n_rows, 1, 6 * D_MODEL)
    bias_all = _bias_tables(rel_bias, max(cfg_p['tq'], cfg_s['tq']))

    xp = x_prompt.reshape(Bp * T, D_MODEL)
    xs = x_sample.reshape(Bs * Ts, D_MODEL)
    kbuf_p, vbuf_p = _empty_context(Bp, BAND_LEFT + T)
    zero_shift = jnp.zeros((Bp, 1, SHIFT_COLS), F32)
    zero_state = jnp.zeros((Bp, NG, GW, GW), F32)
    n_keep = min(BAND_LEFT, T)

    outs = [[] for _ in range(8)]
    for l in range(DEPTH):
        lw_p, lw_s = _layer_weights(l, p, (cfg_p, cfg_s), bias_all)
        final = l == DEPTH - 1
        xp, k_t, v_t, kbuf_p, vbuf_p, st, zs = _layer(
            xp, mod3, l * n_rows // cfg_p['nb'], kbuf_p, vbuf_p, zero_shift, zero_state, lw_p, gf,
            B=Bp, T=T, final=final, **cfg_p)
        outs[0].append(k_t.reshape(Bp, n_keep, N_HEADS, HEAD_DIM))
        outs[1].append(v_t.reshape(Bp, n_keep, N_HEADS, HEAD_DIM))
        outs[2].append(_state_from_groups(st))
        outs[3].append(zs.reshape(Bp, T, SHIFT_COLS)[:, -1])

        pad = ((0, 0), (0, Ts), (0, 0))
        kbuf_s = jnp.pad(cache_attn_k[l].reshape(Bs, W, D_ATTN).astype(BF16), pad)
        vbuf_s = jnp.pad(cache_attn_v[l].reshape(Bs, W, D_ATTN).astype(BF16), pad)
        xs, k_t, v_t, _, _, st, zs = _layer(
            xs, mod3, (l * n_rows + Bp) // cfg_s['nb'], kbuf_s, vbuf_s, state_shift[l].reshape(Bs, 1, SHIFT_COLS),
            _state_to_groups(state_wkv[l]), lw_s, gf, B=Bs, T=Ts, final=final, **cfg_s)
        outs[4].append(k_t.reshape(Bs, Ts, N_HEADS, HEAD_DIM))
        outs[5].append(v_t.reshape(Bs, Ts, N_HEADS, HEAD_DIM))
        outs[6].append(_state_from_groups(st))
        outs[7].append(zs.reshape(Bs, Ts, SHIFT_COLS)[:, -1])

    y_prompt = xp.reshape(Bp, T, D_MODEL)
    y_sample = xs.reshape(Bs, Ts, D_MODEL)
    return (y_prompt, y_sample) + tuple(jnp.stack(o) for o in outs)
```
